```python
import math
import jax, jax.numpy as jnp
from jax import lax
import numpy as np

D_MODEL = 1024
BATCH = 1
SEQ = 16384
DEPTH = 1
DEC_BATCH = 32
DEC_SEQ = 8
PAST_LEN = 16384
PAGE_SIZE = 128

D_MIX = D_MODEL
D_ATTN = D_MIX // 2
N_HEADS = 8
HEAD_DIM = D_ATTN // N_HEADS
N_IDX_HEADS = 8
IDX_DIM = 64
TOPK_MAX = 256
Q_BLOCK = 128
NUM_BUCKETS = 32
MAX_DISTANCE = 128
D_SSM = D_MIX - D_ATTN
SSM_GROUP = 16
N_SSM_GROUPS = D_SSM // SSM_GROUP
SSM_STATE = 64
N_EXPERT_GROUPS = 4
EXPERTS_PER_GROUP = 4
N_EXPERTS = N_EXPERT_GROUPS * EXPERTS_PER_GROUP
TOP_K_EXPERTS = 2
D_EXPERT = 256
N_IN = 3 * D_ATTN + N_IDX_HEADS * IDX_DIM + IDX_DIM + N_IDX_HEADS + D_SSM
ALPHA = (2.0 * DEPTH) ** 0.25
BETA = (8.0 * DEPTH) ** -0.25
LN_EPS = 1e-5
NEG_INF = -1e30

kernel_name = 'hymba_dsa_s5_hmoe_step'


def layer_norm(x, g, b):
    xf = x.astype(jnp.float32)
    mu = jnp.mean(xf, -1, keepdims=True)
    var = jnp.mean(jnp.square(xf - mu), -1, keepdims=True)
    y = (xf - mu) * lax.rsqrt(var + LN_EPS) * g.astype(jnp.float32) + b.astype(jnp.float32)
    return y.astype(x.dtype)


def rel_bucket(dist):
    n = jnp.maximum(dist, 0)
    max_exact = NUM_BUCKETS // 2
    scaled = jnp.log(jnp.maximum(n, 1).astype(jnp.float32) / max_exact) / math.log(MAX_DISTANCE / max_exact)
    large = jnp.minimum(max_exact + (scaled * (NUM_BUCKETS - max_exact)).astype(jnp.int32), NUM_BUCKETS - 1)
    return jnp.where(n < max_exact, n, large)


def gather_rows(rows, idx):
    return jax.vmap(lambda r, i: r[i])(rows, idx)


def project_inputs(x, w_in, idx_ln_g, idx_ln_b):
    B, T, _ = x.shape
    h = x @ w_in
    sizes = (D_ATTN, D_ATTN, D_ATTN, N_IDX_HEADS * IDX_DIM, IDX_DIM, N_IDX_HEADS, D_SSM)
    cuts = [int(c) for c in np.cumsum(sizes)[:-1]]
    q, k, v, q_idx, k_idx, w_idx, u = jnp.split(h, cuts, axis=-1)
    q = q.reshape(B, T, N_HEADS, HEAD_DIM)
    k = k.reshape(B, T, N_HEADS, HEAD_DIM)
    v = v.reshape(B, T, N_HEADS, HEAD_DIM)
    q_idx = q_idx.reshape(B, T, N_IDX_HEADS, IDX_DIM)
    k_idx = layer_norm(k_idx, idx_ln_g, idx_ln_b)
    return q, k, v, q_idx, w_idx, k_idx, u


def indexer_scores(q_idx, w_idx, k_idx):
    s = jnp.einsum('bqhd,bsd->bqsh', q_idx.astype(jnp.float32), k_idx.astype(jnp.float32)) * IDX_DIM ** -0.5
    w = w_idx.astype(jnp.float32) * N_IDX_HEADS ** -0.5
    return jnp.einsum('bqsh,bqh->bqs', jax.nn.relu(s), w)


def select_keys(scores, q_pos, topk):
    key_pos = jnp.arange(scores.shape[-1], dtype=jnp.int32)
    scores = jnp.where(key_pos[None, None, :] <= q_pos[None, :, None], scores, NEG_INF)
    _, idx = lax.top_k(scores, topk)
    return idx


def attend_selected(q, k_sel, v_sel, q_pos, key_pos, rel_bias):
    dist = q_pos[None, :, None] - key_pos
    logits = jnp.einsum('bqhd,bqkhd->bqhk', q.astype(jnp.float32), k_sel.astype(jnp.float32)) * HEAD_DIM ** -0.5
    bias = jnp.transpose(rel_bias.astype(jnp.float32)[rel_bucket(dist)], (0, 1, 3, 2))
    logits = jnp.where((dist >= 0)[:, :, None, :], logits + bias, NEG_INF)
    p = jax.nn.softmax(logits, axis=-1)
    o = jnp.einsum('bqhk,bqkhd->bqhd', p, v_sel.astype(jnp.float32))
    return o.astype(q.dtype)


def prompt_sparse_attention(q, k, v, q_idx, w_idx, k_idx, rel_bias):
    B, T = q.shape[:2]
    topk = min(TOPK_MAX, T // 4)
    nb = T // Q_BLOCK

    def block(args):
        qb, qib, wb, start = args
        q_pos = start + jnp.arange(Q_BLOCK, dtype=jnp.int32)
        idx = select_keys(indexer_scores(qib, wb, k_idx), q_pos, topk)
        return attend_selected(qb, gather_rows(k, idx), gather_rows(v, idx), q_pos, idx, rel_bias)

    def to_blocks(a):
        return jnp.moveaxis(a.reshape((B, nb, Q_BLOCK) + a.shape[2:]), 1, 0)

    starts = jnp.arange(nb, dtype=jnp.int32) * Q_BLOCK
    out = lax.map(block, (to_blocks(q), to_blocks(q_idx), to_blocks(w_idx), starts))
    return jnp.moveaxis(out, 0, 1).reshape(B, T, N_HEADS, HEAD_DIM)


def sample_sparse_attention(q, k_new, v_new, q_idx, w_idx, k_idx_new, cache_k, cache_v, cache_kidx, page_table, rel_bias):
    DB, S_new = q.shape[:2]
    n_pages = page_table.shape[1]
    past = n_pages * PAGE_SIZE
    topk = min(TOPK_MAX, (past + S_new) // 4)
    kidx_past = cache_kidx[page_table].reshape(DB, past, IDX_DIM)
    kidx_all = jnp.concatenate([kidx_past, k_idx_new.astype(kidx_past.dtype)], axis=1)
    q_pos = past + jnp.arange(S_new, dtype=jnp.int32)
    idx = select_keys(indexer_scores(q_idx, w_idx, kidx_all), q_pos, topk)
    from_new = (idx >= past)[..., None, None]
    idx_past = jnp.minimum(idx, past - 1)
    phys = gather_rows(page_table, idx_past // PAGE_SIZE)
    off = idx_past % PAGE_SIZE
    idx_new = jnp.clip(idx - past, 0, S_new - 1)
    k_sel = jnp.where(from_new, gather_rows(k_new, idx_new), cache_k[phys, off].astype(k_new.dtype))
    v_sel = jnp.where(from_new, gather_rows(v_new, idx_new), cache_v[phys, off].astype(v_new.dtype))
    return attend_selected(q, k_sel, v_sel, q_pos, idx, rel_bias)


def ssm_combine(e1, e2):
    ar1, ai1, br1, bi1 = e1
    ar2, ai2, br2, bi2 = e2
    return (ar2 * ar1 - ai2 * ai1, ar2 * ai1 + ai2 * ar1,
            ar2 * br1 - ai2 * bi1 + br2, ar2 * bi1 + ai2 * br1 + bi2)


def s5_mixer(u, h0_re, h0_im, lam_re, lam_im, log_dt, b_re, b_im, c_re, c_im, d_skip, w_glu):
    f32 = jnp.float32
    B, T, _ = u.shape
    uf = u.astype(f32).reshape(B, T, N_SSM_GROUPS, SSM_GROUP)
    lr, li = lam_re.astype(f32), lam_im.astype(f32)
    dt = jnp.exp(log_dt.astype(f32))[:, None]
    mag = jnp.exp(lr * dt)
    lbar_re, lbar_im = mag * jnp.cos(li * dt), mag * jnp.sin(li * dt)
    den = lr * lr + li * li
    xr, xi = lbar_re - 1.0, lbar_im
    coef_re = (xr * lr + xi * li) / den
    coef_im = (xi * lr - xr * li) / den
    br, bi = b_re.astype(f32), b_im.astype(f32)
    bbar_re = coef_re[..., None] * br - coef_im[..., None] * bi
    bbar_im = coef_re[..., None] * bi + coef_im[..., None] * br
    bu_re = jnp.einsum('btgc,gpc->btgp', uf, bbar_re)
    bu_im = jnp.einsum('btgc,gpc->btgp', uf, bbar_im)
    h0r, h0i = h0_re.astype(f32), h0_im.astype(f32)
    bu_re = bu_re.at[:, 0].add(lbar_re * h0r - lbar_im * h0i)
    bu_im = bu_im.at[:, 0].add(lbar_re * h0i + lbar_im * h0r)
    a_re = jnp.broadcast_to(lbar_re, bu_re.shape)
    a_im = jnp.broadcast_to(lbar_im, bu_im.shape)
    _, _, h_re, h_im = lax.associative_scan(ssm_combine, (a_re, a_im, bu_re, bu_im), axis=1)
    y = (jnp.einsum('btgp,gcp->btgc', h_re, c_re.astype(f32))
         - jnp.einsum('btgp,gcp->btgc', h_im, c_im.astype(f32))
         + d_skip.astype(f32).reshape(N_SSM_GROUPS, SSM_GROUP) * uf)
    z = jax.nn.gelu(y.reshape(B, T, D_SSM))
    val, gate = jnp.split(z @ w_glu.astype(f32), 2, axis=-1)
    out = val * jax.nn.sigmoid(gate)
    return out.astype(u.dtype), h_re[:, -1], h_im[:, -1]


def hier_moe(x, w_rg, b_rg, w_re, b_re, w_gate, w_up, w_down):
    f32 = jnp.float32
    N = x.shape[0]
    grp_logits = (x @ w_rg).astype(f32) + b_rg.astype(f32)
    grp_prob = jax.nn.softmax(grp_logits, axis=-1)
    g_sel = jnp.argmax(grp_logits, axis=-1)
    p_grp = jnp.max(grp_prob, axis=-1)
    exp_logits_all = jnp.einsum('nd,gde->nge', x, w_re).astype(f32) + b_re.astype(f32)[None]
    exp_logits = jnp.take_along_axis(exp_logits_all, g_sel[:, None, None], axis=1)[:, 0]
    top_v, top_i = lax.top_k(exp_logits, TOP_K_EXPERTS)
    top_w = jax.nn.softmax(top_v, axis=-1) * p_grp[:, None]
    expert_id = g_sel[:, None] * EXPERTS_PER_GROUP + top_i
    gates = jnp.sum(jax.nn.one_hot(expert_id, N_EXPERTS, dtype=f32) * top_w[..., None], axis=1)
    hidden = jax.nn.silu(jnp.einsum('nd,edf->nef', x, w_gate)) * jnp.einsum('nd,edf->nef', x, w_up)
    y = jnp.einsum('nef,efd->nd', hidden * gates[..., None].astype(hidden.dtype), w_down)
    return y.astype(x.dtype)


def finish_layer(x, attn, ssm, w_out, g1, b1, w_rg, b_rg, w_re, b_re, w_gate, w_up, w_down, g2, b2):
    B, T, _ = x.shape
    mix = jnp.concatenate([attn.reshape(B, T, D_ATTN), ssm.astype(attn.dtype)], axis=-1) @ w_out
    x = layer_norm(ALPHA * x + mix, g1, b1)
    y = hier_moe(x.reshape(B * T, D_MODEL), w_rg, b_rg, w_re, b_re, w_gate, w_up, w_down)
    return layer_norm(ALPHA * x + y.reshape(B, T, D_MODEL), g2, b2)


def setup_inputs(seed: int = 0) -> dict:
    key = jax.random.key(seed)
    ks = iter(jax.random.split(key, 48))
    f32 = jnp.float32
    nrm = lambda shape, scale=1.0: scale * jax.random.normal(next(ks), shape, f32)
    n_pages = PAST_LEN // PAGE_SIZE
    used = DEC_BATCH * n_pages
    n_pool = used + max(1, used // 4)
    page_table = jax.random.permutation(next(ks), n_pool)[:used].reshape(DEC_BATCH, n_pages).astype(jnp.int32)
    lam_im = jnp.pi * jnp.arange(SSM_STATE, dtype=f32)
    return {
        'x_prompt': nrm((BATCH, SEQ, D_MODEL)),
        'x_sample': nrm((DEC_BATCH, DEC_SEQ, D_MODEL)),
        'cache_k': nrm((DEPTH, n_pool, PAGE_SIZE, N_HEADS, HEAD_DIM)),
        'cache_v': nrm((DEPTH, n_pool, PAGE_SIZE, N_HEADS, HEAD_DIM)),
        'cache_kidx': nrm((DEPTH, n_pool, PAGE_SIZE, IDX_DIM)),
        'state_ssm_re': nrm((DEPTH, DEC_BATCH, N_SSM_GROUPS, SSM_STATE), 0.1),
        'state_ssm_im': nrm((DEPTH, DEC_BATCH, N_SSM_GROUPS, SSM_STATE), 0.1),
        'page_table': page_table,
        'ln_in_g': 1.0 + nrm((D_MODEL,), 0.02),
        'ln_in_b': nrm((D_MODEL,), 0.02),
        'rel_bias': nrm((NUM_BUCKETS, N_HEADS), 0.1),
        'w_in': nrm((DEPTH, D_MODEL, N_IN), D_MODEL ** -0.5),
        'idx_ln_g': 1.0 + nrm((DEPTH, IDX_DIM), 0.02),
        'idx_ln_b': nrm((DEPTH, IDX_DIM), 0.02),
        'ssm_lam_re': -0.5 + nrm((DEPTH, N_SSM_GROUPS, SSM_STATE), 0.01),
        'ssm_lam_im': lam_im[None, None, :] + nrm((DEPTH, N_SSM_GROUPS, SSM_STATE), 0.01),
        'ssm_log_dt': jax.random.uniform(next(ks), (DEPTH, N_SSM_GROUPS), f32, math.log(1e-3), math.log(1e-1)),
        'ssm_b_re': nrm((DEPTH, N_SSM_GROUPS, SSM_STATE, SSM_GROUP), (2 * SSM_GROUP) ** -0.5),
        'ssm_b_im': nrm((DEPTH, N_SSM_GROUPS, SSM_STATE, SSM_GROUP), (2 * SSM_GROUP) ** -0.5),
        'ssm_c_re': nrm((DEPTH, N_SSM_GROUPS, SSM_GROUP, SSM_STATE), SSM_STATE ** -0.25),
        'ssm_c_im': nrm((DEPTH, N_SSM_GROUPS, SSM_GROUP, SSM_STATE), SSM_STATE ** -0.25),
        'ssm_d': nrm((DEPTH, D_SSM)),
        'w_glu': nrm((DEPTH, D_SSM, 2 * D_SSM), D_SSM ** -0.5),
        'w_out': nrm((DEPTH, D_MIX, D_MODEL), BETA * D_MIX ** -0.5),
        'ln_mix_g': 1.0 + nrm((DEPTH, D_MODEL), 0.02),
        'ln_mix_b': nrm((DEPTH, D_MODEL), 0.02),
        'w_router_grp': nrm((DEPTH, D_MODEL, N_EXPERT_GROUPS), D_MODEL ** -0.5),
        'b_router_grp': nrm((DEPTH, N_EXPERT_GROUPS), 0.01),
        'w_router_exp': nrm((DEPTH, N_EXPERT_GROUPS, D_MODEL, EXPERTS_PER_GROUP), D_MODEL ** -0.5),
        'b_router_exp': nrm((DEPTH, N_EXPERT_GROUPS, EXPERTS_PER_GROUP), 0.01),
        'w_gate': nrm((DEPTH, N_EXPERTS, D_MODEL, D_EXPERT), D_MODEL ** -0.5),
        'w_up': nrm((DEPTH, N_EXPERTS, D_MODEL, D_EXPERT), D_MODEL ** -0.5),
        'w_down': nrm((DEPTH, N_EXPERTS, D_EXPERT, D_MODEL), BETA * D_EXPERT ** -0.5),
        'ln_ffn_g': 1.0 + nrm((DEPTH, D_MODEL), 0.02),
        'ln_ffn_b': nrm((DEPTH, D_MODEL), 0.02),
    }


def reference(x_prompt, x_sample, cache_k, cache_v, cache_kidx, state_ssm_re, state_ssm_im, page_table,
              ln_in_g, ln_in_b, rel_bias, w_in, idx_ln_g, idx_ln_b, ssm_lam_re, ssm_lam_im, ssm_log_dt,
              ssm_b_re, ssm_b_im, ssm_c_re, ssm_c_im, ssm_d, w_glu, w_out, ln_mix_g, ln_mix_b,
              w_router_grp, b_router_grp, w_router_exp, b_router_exp, w_gate, w_up, w_down,
              ln_ffn_g, ln_ffn_b):
    xp = layer_norm(x_prompt, ln_in_g, ln_in_b)
    xs = layer_norm(x_sample, ln_in_g, ln_in_b)
    kp, vp, kip, hrp, hip = [], [], [], [], []
    ksm, vsm, kis, hrs, his = [], [], [], [], []
    for l in range(DEPTH):
        ssm_p = (ssm_lam_re[l], ssm_lam_im[l], ssm_log_dt[l], ssm_b_re[l], ssm_b_im[l],
                 ssm_c_re[l], ssm_c_im[l], ssm_d[l], w_glu[l])
        post_p = (w_out[l], ln_mix_g[l], ln_mix_b[l], w_router_grp[l], b_router_grp[l],
                  w_router_exp[l], b_router_exp[l], w_gate[l], w_up[l], w_down[l], ln_ffn_g[l], ln_ffn_b[l])
        q, k, v, qi, wi, ki, u = project_inputs(xp, w_in[l], idx_ln_g[l], idx_ln_b[l])
        attn = prompt_sparse_attention(q, k, v, qi, wi, ki, rel_bias)
        h0 = jnp.zeros((xp.shape[0], N_SSM_GROUPS, SSM_STATE), jnp.float32)
        ssm, hr, hi = s5_mixer(u, h0, h0, *ssm_p)
        xp = finish_layer(xp, attn, ssm, *post_p)
        kp.append(k); vp.append(v); kip.append(ki)
        hrp.append(hr.astype(state_ssm_re.dtype)); hip.append(hi.astype(state_ssm_im.dtype))
        q, k, v, qi, wi, ki, u = project_inputs(xs, w_in[l], idx_ln_g[l], idx_ln_b[l])
        attn = sample_sparse_attention(q, k, v, qi, wi, ki, cache_k[l], cache_v[l], cache_kidx[l], page_table, rel_bias)
        ssm, hr, hi = s5_mixer(u, state_ssm_re[l], state_ssm_im[l], *ssm_p)
        xs = finish_layer(xs, attn, ssm, *post_p)
        ksm.append(k); vsm.append(v); kis.append(ki)
        hrs.append(hr.astype(state_ssm_re.dtype)); his.append(hi.astype(state_ssm_im.dtype))
    return (xp, xs,
            jnp.stack(kp), jnp.stack(vp), jnp.stack(kip), jnp.stack(hrp), jnp.stack(hip),
            jnp.stack(ksm), jnp.stack(vsm), jnp.stack(kis), jnp.stack(hrs), jnp.stack(his))
```

```python
import functools
import math

import numpy as np
import jax
import jax.numpy as jnp
from jax import lax
from jax.experimental import pallas as pl
from jax.experimental.pallas import tpu as pltpu

N_HEADS = 8
HEAD_DIM = 64
N_IDX_HEADS = 8
IDX_DIM = 64
TOPK_MAX = 256
NUM_BUCKETS = 32
MAX_DISTANCE = 128
SSM_GROUP = 16
SSM_STATE = 64
N_EXPERT_GROUPS = 4
EXPERTS_PER_GROUP = 4
N_EXPERTS = N_EXPERT_GROUPS * EXPERTS_PER_GROUP
LN_EPS = 1e-5
NEG_INF = -1e30
LOG2E = math.log2(math.e)

LANES = 128
VMEM_LIMIT = 56 * 1024 * 1024

F32 = jnp.float32
BF16 = jnp.bfloat16


def _cparams(n_grid, vmem=VMEM_LIMIT):
    return pltpu.CompilerParams(dimension_semantics=("arbitrary",) * n_grid, vmem_limit_bytes=vmem)


def _resident(shape):
    nd = len(shape)
    return pl.BlockSpec(shape, lambda *_: (0,) * nd, pipeline_mode=pl.Buffered(1))


def _layer_norm(x, g, b):
    mu = jnp.mean(x, axis=-1, keepdims=True)
    xc = x - mu
    var = jnp.mean(xc * xc, axis=-1, keepdims=True)
    return xc * lax.rsqrt(var + LN_EPS) * g + b


def _proj_kernel(x_ref, g_ref, b_ref, w_ref, ig_ref, ib_ref,
                 q_ref, k_ref, v_ref, kb_ref, vb_ref, qi_ref, u_ref, misc_ref, *, d_attn, d_ssm, pre_ln):
    xn = x_ref[...]
    if pre_ln:
        xn = _layer_norm(xn, g_ref[...], b_ref[...])
    h = jnp.dot(xn.astype(BF16), w_ref[...], preferred_element_type=F32)
    o = 0
    q = h[:, o:o + d_attn]; o += d_attn
    k = h[:, o:o + d_attn]; o += d_attn
    v = h[:, o:o + d_attn]; o += d_attn
    qi = h[:, o:o + N_IDX_HEADS * IDX_DIM]; o += N_IDX_HEADS * IDX_DIM
    u = h[:, o:o + d_ssm]; o += d_ssm
    slab = h[:, o:o + LANES]
    q_ref[...] = (q * (HEAD_DIM ** -0.5 * LOG2E)).astype(BF16)
    k_ref[...] = k
    v_ref[...] = v
    kb_ref[...] = k.astype(BF16)
    vb_ref[...] = v.astype(BF16)
    qi_ref[...] = (qi * IDX_DIM ** -0.5).astype(BF16)
    u_ref[...] = u
    lane = lax.broadcasted_iota(jnp.int32, slab.shape, 1)
    is_ki = lane < IDX_DIM
    mu = jnp.sum(jnp.where(is_ki, slab, 0.0), axis=-1, keepdims=True) / IDX_DIM
    kc = jnp.where(is_ki, slab - mu, 0.0)
    var = jnp.sum(kc * kc, axis=-1, keepdims=True) / IDX_DIM
    ki = kc * lax.rsqrt(var + LN_EPS) * ig_ref[...] + ib_ref[...]
    misc_ref[...] = jnp.where(is_ki, ki, slab * N_IDX_HEADS ** -0.5)


def _proj_call(x, ln_g, ln_b, w_all, ig, ib, d_attn, d_ssm, pre_ln, tm):
    n, d = x.shape
    nw = w_all.shape[1]
    row = lambda c: pl.BlockSpec((tm, c), lambda i: (i, 0))
    out_shape = (
        jax.ShapeDtypeStruct((n, d_attn), BF16),
        jax.ShapeDtypeStruct((n, d_attn), F32),
        jax.ShapeDtypeStruct((n, d_attn), F32),
        jax.ShapeDtypeStruct((n, d_attn), BF16),
        jax.ShapeDtypeStruct((n, d_attn), BF16),
        jax.ShapeDtypeStruct((n, N_IDX_HEADS * IDX_DIM), BF16),
        jax.ShapeDtypeStruct((n, d_ssm), F32),
        jax.ShapeDtypeStruct((n, LANES), F32),
    )
    return pl.pallas_call(
        functools.partial(_proj_kernel, d_attn=d_attn, d_ssm=d_ssm, pre_ln=pre_ln),
        grid=(n // tm,),
        in_specs=[row(d), _resident((1, d)), _resident((1, d)), _resident((d, nw)),
                  _resident((1, LANES)), _resident((1, LANES))],
        out_specs=(row(d_attn), row(d_attn), row(d_attn), row(d_attn), row(d_attn),
                   row(N_IDX_HEADS * IDX_DIM), row(d_ssm), row(LANES)),
        out_shape=out_shape,
        compiler_params=_cparams(1),
        name="proj",
    )(x, ln_g, ln_b, w_all, ig, ib)


def _rel_bucket(dist):
    n = jnp.maximum(dist, 0)
    max_exact = NUM_BUCKETS // 2
    scaled = jnp.log(jnp.maximum(n, 1).astype(F32) / max_exact) / math.log(MAX_DISTANCE / max_exact)
    large = jnp.minimum(max_exact + (scaled * (NUM_BUCKETS - max_exact)).astype(jnp.int32), NUM_BUCKETS - 1)
    return jnp.where(n < max_exact, n, large)


def _shifted_bias(dist, rb_ref, h):
    bkt = _rel_bucket(dist)
    far = rb_ref[NUM_BUCKETS - 1, h]
    out = jnp.zeros(dist.shape, F32)
    for b in range(NUM_BUCKETS - 1):
        out = jnp.where(bkt == b, rb_ref[b, h] - far, out)
    return out * LOG2E


def _lane_tile_sum(x):
    acc = x[:, :LANES]
    for t in range(1, x.shape[1] // LANES):
        acc = acc + x[:, t * LANES:(t + 1) * LANES]
    return acc


def _lane_tile_max(x):
    acc = x[:, :LANES]
    for t in range(1, x.shape[1] // LANES):
        acc = jnp.maximum(acc, x[:, t * LANES:(t + 1) * LANES])
    return acc


def _lane_tile_min(x):
    acc = x[:, :LANES]
    for t in range(1, x.shape[1] // LANES):
        acc = jnp.minimum(acc, x[:, t * LANES:(t + 1) * LANES])
    return acc


def _topk_threshold(sc_ref, nch, kb, k_row, rmin, rmax, n_bisect, n_index_bits):
    rows = sc_ref.shape[0]

    def chunk(c):
        return sc_ref[:, pl.ds(pl.multiple_of(c * kb, kb), kb)]

    def count_ge(t):
        def body(c, cnt):
            return cnt + _lane_tile_sum(jnp.where(chunk(c) >= t, 1.0, 0.0))
        cnt = lax.fori_loop(0, nch, body, jnp.zeros((rows, LANES), F32))
        return jnp.sum(cnt, axis=-1, keepdims=True)

    def bisect(_, lohi):
        lo, hi = lohi
        mid = 0.5 * (lo + hi)
        ok = count_ge(mid) >= k_row
        return jnp.where(ok, mid, lo), jnp.where(ok, hi, mid)

    _, hi = lax.fori_loop(0, n_bisect, bisect, (rmin, rmax))

    def count_ge_and_next(m):
        def body(c, carry):
            cnt, nxt = carry
            x = chunk(c)
            cnt = cnt + _lane_tile_sum(jnp.where(x >= m, 1.0, 0.0))
            nxt = jnp.maximum(nxt, _lane_tile_max(jnp.where(x < m, x, NEG_INF)))
            return cnt, nxt
        cnt, nxt = lax.fori_loop(0, nch, body, (jnp.zeros((rows, LANES), F32),
                                                jnp.full((rows, LANES), NEG_INF, F32)))
        return jnp.sum(cnt, axis=-1, keepdims=True), jnp.max(nxt, axis=-1, keepdims=True)

    def first_le(t):
        def body(c, nxt):
            x = chunk(c)
            return jnp.maximum(nxt, _lane_tile_max(jnp.where(x <= t, x, NEG_INF)))
        nxt = lax.fori_loop(0, nch, body, jnp.full((rows, LANES), NEG_INF, F32))
        return jnp.max(nxt, axis=-1, keepdims=True)

    def walk_cond(st):
        return st[3] > 0.0

    def walk_body(st):
        m, cfin, done, _ = st
        cnt, nxt = count_ge_and_next(m)
        ok = jnp.logical_and(cnt >= k_row, done < 0.5)
        cfin = jnp.where(ok, cnt, cfin)
        done = jnp.where(cnt >= k_row, 1.0, done)
        m = jnp.where(done > 0.5, m, nxt)
        return m, cfin, done, jnp.sum(1.0 - done)

    zeros = jnp.zeros((rows, 1), F32)
    thr, cfin, _, _ = lax.while_loop(walk_cond, walk_body, (first_le(hi), zeros, zeros, jnp.float32(rows)))

    @pl.when(jnp.sum(jnp.where(cfin > k_row, 1.0, 0.0)) > 0.0)
    def _():
        def count_gt():
            def body(c, cnt):
                return cnt + _lane_tile_sum(jnp.where(chunk(c) > thr, 1.0, 0.0))
            cnt = lax.fori_loop(0, nch, body, jnp.zeros((rows, LANES), F32))
            return jnp.sum(cnt, axis=-1, keepdims=True)

        need = k_row - count_gt()

        def count_eq_upto(j):
            def body(c, cnt):
                x = chunk(c)
                idx = (c * kb + lax.broadcasted_iota(jnp.int32, x.shape, 1)).astype(F32)
                hit = jnp.where(x == thr, jnp.where(idx <= j, 1.0, 0.0), 0.0)
                return cnt + _lane_tile_sum(hit)
            cnt = lax.fori_loop(0, nch, body, jnp.zeros((rows, LANES), F32))
            return jnp.sum(cnt, axis=-1, keepdims=True)

        def ibisect(_, lohi):
            lo, hi_i = lohi
            mid = jnp.floor(0.5 * (lo + hi_i))
            ok = count_eq_upto(mid) >= need
            return jnp.where(ok, lo, mid), jnp.where(ok, mid, hi_i)

        width = jnp.float32(1.0) * (nch * kb)
        _, cut = lax.fori_loop(0, n_index_bits, ibisect,
                               (jnp.full((rows, 1), -1.0, F32), jnp.zeros((rows, 1), F32) + (width - 1.0)))

        def drop(c, _):
            off = pl.multiple_of(c * kb, kb)
            x = sc_ref[:, pl.ds(off, kb)]
            idx = (c * kb + lax.broadcasted_iota(jnp.int32, x.shape, 1)).astype(F32)
            dropped = jnp.where(x == thr, jnp.where(idx > cut, 1.0, 0.0), 0.0)
            sc_ref[:, pl.ds(off, kb)] = jnp.where(dropped > 0.5, NEG_INF, x)
            return 0
        lax.fori_loop(0, nch, drop, 0)

    return thr


def _pattn_kernel(rb_ref, q_ref, qi_ref, misc_ref, kit_ref, kb_ref, vb_ref, o_ref,
                  sc_ref, qm_ref, qim_ref, bias_ref, m_ref, l_ref, acc_ref, *, tq, kb, n_bisect, n_index_bits):
    i = pl.program_id(1)
    n_pairs = N_HEADS // 2
    nch = ((i + 1) * tq + kb - 1) // kb
    sub = kb // tq
    lane = lax.broadcasted_iota(jnp.int32, (tq, LANES), 1)
    lo_half = lane < HEAD_DIM
    q_pos = i * tq + lax.broadcasted_iota(jnp.int32, (tq, 1), 0)

    @pl.when(i == 0)
    def _():
        r = lax.broadcasted_iota(jnp.int32, (tq, tq), 0)
        c = lax.broadcasted_iota(jnp.int32, (tq, tq), 1)
        for h in range(N_HEADS):
            bias_ref[0, h] = jnp.zeros((tq, tq), F32)
            bias_ref[1, h] = _shifted_bias(r - c + tq, rb_ref, h)
            bias_ref[2, h] = _shifted_bias(r - c, rb_ref, h)

    for h in range(N_HEADS):
        keep = lo_half if h % 2 == 0 else jnp.logical_not(lo_half)
        pr = h // 2
        qm_ref[h] = jnp.where(keep, q_ref[:, pr * LANES:(pr + 1) * LANES].astype(F32), 0.0).astype(BF16)
        qim_ref[h] = jnp.where(keep, qi_ref[:, pr * LANES:(pr + 1) * LANES].astype(F32), 0.0).astype(BF16)

    wcols = [misc_ref[:, IDX_DIM + h:IDX_DIM + h + 1] for h in range(N_IDX_HEADS)]

    def score_chunk(c, carry):
        rmin, rmax = carry
        off = pl.multiple_of(c * kb, kb)
        kt = kit_ref[:, pl.ds(off, kb)]
        acc = jnp.zeros((tq, kb), F32)
        for h in range(N_IDX_HEADS):
            s = jnp.dot(qim_ref[h], kt, preferred_element_type=F32)
            acc = acc + jnp.maximum(s, 0.0) * wcols[h]
        key_pos = off + lax.broadcasted_iota(jnp.int32, (tq, kb), 1)
        valid = key_pos <= q_pos
        sc_ref[:, pl.ds(off, kb)] = jnp.where(valid, acc, NEG_INF)
        rmax = jnp.maximum(rmax, _lane_tile_max(jnp.where(valid, acc, NEG_INF)))
        rmin = jnp.minimum(rmin, _lane_tile_min(jnp.where(valid, acc, -NEG_INF)))
        return rmin, rmax

    rmin, rmax = lax.fori_loop(0, nch, score_chunk,
                               (jnp.full((tq, LANES), -NEG_INF, F32), jnp.full((tq, LANES), NEG_INF, F32)))
    rmin = jnp.min(rmin, axis=-1, keepdims=True)
    rmax = jnp.max(rmax, axis=-1, keepdims=True)

    topk = min(TOPK_MAX, kit_ref.shape[1] // 4)
    k_row = jnp.minimum(q_pos + 1, topk).astype(F32)
    thr = _topk_threshold(sc_ref, nch, kb, k_row, rmin, rmax, n_bisect, n_index_bits)

    m_ref[...] = jnp.full(m_ref.shape, NEG_INF, F32)
    l_ref[...] = jnp.zeros(l_ref.shape, F32)
    acc_ref[...] = jnp.zeros(acc_ref.shape, F32)

    def attend_chunk(c, with_bias):
        off = pl.multiple_of(c * kb, kb)
        madd = jnp.where(sc_ref[:, pl.ds(off, kb)] >= thr, 0.0, NEG_INF)
        for pr in range(n_pairs):
            kp = kb_ref[pl.ds(off, kb), pr * LANES:(pr + 1) * LANES]
            vp = vb_ref[pl.ds(off, kb), pr * LANES:(pr + 1) * LANES]
            for hh in range(2):
                h = 2 * pr + hh
                s = lax.dot_general(qm_ref[h], kp, (((1,), (1,)), ((), ())), preferred_element_type=F32)
                s = s + madd
                if with_bias:
                    tiles = []
                    for t in range(sub):
                        which = jnp.clip(c * sub + t - i + 2, 0, 2)
                        tiles.append(bias_ref[which, h])
                    s = s + jnp.concatenate(tiles, axis=1)
                m_prev = m_ref[h]
                m_next = jnp.maximum(m_prev, jnp.max(s, axis=-1, keepdims=True))
                alpha = jnp.exp2(m_prev - m_next)
                p = jnp.exp2(s - m_next[:, :1])
                l_ref[h] = alpha * l_ref[h] + _lane_tile_sum(p)
                m_ref[h] = m_next
                pv = jnp.dot(p.astype(BF16), vp, preferred_element_type=F32)
                mine = lo_half if hh == 0 else jnp.logical_not(lo_half)
                acc_ref[pr] = jnp.where(mine, acc_ref[pr] * alpha + pv, acc_ref[pr])

    n_far = jnp.maximum((i * tq - MAX_DISTANCE + 1) // kb, 0)

    def far_body(c, _):
        attend_chunk(c, False)
        return 0

    def near_body(c, _):
        attend_chunk(c, True)
        return 0

    lax.fori_loop(0, n_far, far_body, 0)
    lax.fori_loop(n_far, nch, near_body, 0)

    for pr in range(n_pairs):
        l0 = jnp.sum(l_ref[2 * pr], axis=-1, keepdims=True)
        l1 = jnp.sum(l_ref[2 * pr + 1], axis=-1, keepdims=True)
        o_ref[:, pr * LANES:(pr + 1) * LANES] = acc_ref[pr] / jnp.where(lo_half, l0, l1)


def _pattn_call(rel_bias, q, qi, misc, kit, kb16, vb16, tq, kb):
    bsz, s, d_attn = q.shape
    assert tq == LANES and MAX_DISTANCE <= tq and s % kb == 0 and kb % tq == 0
    n_bisect = 16
    n_index_bits = int(math.ceil(math.log2(s))) + 1
    blk = lambda c: pl.BlockSpec((None, tq, c), lambda b, i: (b, i, 0))
    whole = lambda r, c: pl.BlockSpec((None, r, c), lambda b, i: (b, 0, 0), pipeline_mode=pl.Buffered(1))
    return pl.pallas_call(
        functools.partial(_pattn_kernel, tq=tq, kb=kb, n_bisect=n_bisect, n_index_bits=n_index_bits),
        grid=(bsz, s // tq),
        in_specs=[pl.BlockSpec(memory_space=pltpu.SMEM),
                  blk(d_attn), blk(N_IDX_HEADS * IDX_DIM), blk(LANES),
                  whole(LANES, s), whole(s, d_attn), whole(s, d_attn)],
        out_specs=blk(d_attn),
        out_shape=jax.ShapeDtypeStruct((bsz, s, d_attn), F32),
        scratch_shapes=[
            pltpu.VMEM((tq, s), F32),
            pltpu.VMEM((N_HEADS, tq, LANES), BF16),
            pltpu.VMEM((N_IDX_HEADS, tq, LANES), BF16),
            pltpu.VMEM((3, N_HEADS, tq, tq), F32),
            pltpu.VMEM((N_HEADS, tq, LANES), F32),
            pltpu.VMEM((N_HEADS, tq, LANES), F32),
            pltpu.VMEM((N_HEADS // 2, tq, LANES), F32),
        ],
        compiler_params=_cparams(2),
        name="pattn",
    )(rel_bias, q, qi, misc, kit, kb16, vb16)


def _sattn_kernel(pt_ref, rb_ref, q_ref, qis_ref, w_ref, kin_ref, kn_ref, vn_ref,
                  cki_hbm, ck_hbm, cv_hbm, o_ref,
                  sc_ref, ibuf, kbuf, vbuf, isem, ksem, vsem, *, cp, page, n_bisect, n_index_bits):
    b = pl.program_id(0)
    n_pages = pt_ref.shape[1]
    nc = n_pages // cp
    ck = cp * page
    past = n_pages * page
    sn = q_ref.shape[0]
    rows = N_HEADS * sn
    d_attn = q_ref.shape[1]

    def page_copies(cache, buf, sem, c, slot):
        return [pltpu.make_async_copy(cache.at[pt_ref[b, c * cp + j]],
                                      buf.at[slot, pl.ds(j * page, page)], sem.at[slot])
                for j in range(cp)]

    def start(copies):
        for cpy in copies:
            cpy.start()

    def wait(copies):
        for cpy in copies:
            cpy.wait()

    start(page_copies(cki_hbm, ibuf, isem, 0, 0))
    start(page_copies(ck_hbm, kbuf, ksem, 0, 0))
    start(page_copies(cv_hbm, vbuf, vsem, 0, 0))

    q_pos = past + lax.broadcasted_iota(jnp.int32, (sn, 1), 0)
    qis = qis_ref[...]
    w = w_ref[...]

    def head_sum(s):
        acc = jnp.zeros((sn, s.shape[1]), F32)
        for h in range(N_IDX_HEADS):
            acc = acc + jnp.maximum(s[h * sn:(h + 1) * sn], 0.0) * w[h * sn:(h + 1) * sn]
        return acc

    rmin = jnp.full((sn, LANES), -NEG_INF, F32)
    rmax = jnp.full((sn, LANES), NEG_INF, F32)
    for c in range(nc):
        slot = c % 2
        if c + 1 < nc:
            start(page_copies(cki_hbm, ibuf, isem, c + 1, 1 - slot))
        wait(page_copies(cki_hbm, ibuf, isem, c, slot))
        kc = ibuf[slot].astype(BF16)
        s = lax.dot_general(qis, kc, (((1,), (1,)), ((), ())), preferred_element_type=F32)
        acc = head_sum(s)
        sc_ref[:, c * ck:(c + 1) * ck] = acc
        rmax = jnp.maximum(rmax, _lane_tile_max(acc))
        rmin = jnp.minimum(rmin, _lane_tile_min(acc))
    s = lax.dot_general(qis, kin_ref[...], (((1,), (1,)), ((), ())), preferred_element_type=F32)
    acc = head_sum(s)
    lane = lax.broadcasted_iota(jnp.int32, (sn, LANES), 1)
    valid = jnp.logical_and(lane < sn, past + lane <= q_pos)
    sc_ref[:, past:past + LANES] = jnp.where(valid, acc, NEG_INF)
    sc_ref[:, past + LANES:past + ck] = jnp.full((sn, ck - LANES), NEG_INF, F32)
    rmax = jnp.max(jnp.maximum(rmax, jnp.where(valid, acc, NEG_INF)), axis=-1, keepdims=True)
    rmin = jnp.min(jnp.minimum(rmin, jnp.where(valid, acc, -NEG_INF)), axis=-1, keepdims=True)

    topk = min(TOPK_MAX, (past + sn) // 4)
    k_row = jnp.minimum(q_pos + 1, topk).astype(F32)
    thr = _topk_threshold(sc_ref, nc + 1, ck, k_row, rmin, rmax, n_bisect, n_index_bits)

    r_head = lax.broadcasted_iota(jnp.int32, (rows, d_attn), 0) // sn
    c_head = lax.broadcasted_iota(jnp.int32, (rows, d_attn), 1) // HEAD_DIM
    own = r_head == c_head
    qbd = jnp.where(own, jnp.tile(q_ref[...].astype(F32), (N_HEADS, 1)), 0.0).astype(BF16)

    def softmax_step(state, s, vals):
        m_prev, l_prev, acc_prev = state
        m_next = jnp.maximum(m_prev, jnp.max(s, axis=-1, keepdims=True))
        alpha = jnp.exp2(m_prev - m_next)
        p = jnp.exp2(s - m_next[:, :1])
        l_next = alpha * l_prev + _lane_tile_sum(p)
        pv = jnp.dot(p.astype(BF16), vals, preferred_element_type=F32)
        return m_next, l_next, acc_prev * alpha[:, :1] + pv

    def biased(s, key0):
        dist = q_pos - (key0 + lax.broadcasted_iota(jnp.int32, (sn, s.shape[1]), 1))
        return s + jnp.concatenate([_shifted_bias(dist, rb_ref, h) for h in range(N_HEADS)], axis=0)

    state = (jnp.full((rows, LANES), NEG_INF, F32), jnp.zeros((rows, LANES), F32),
             jnp.zeros((rows, d_attn), F32))
    for c in range(nc):
        slot = c % 2
        if c + 1 < nc:
            start(page_copies(ck_hbm, kbuf, ksem, c + 1, 1 - slot))
            start(page_copies(cv_hbm, vbuf, vsem, c + 1, 1 - slot))
        wait(page_copies(ck_hbm, kbuf, ksem, c, slot))
        wait(page_copies(cv_hbm, vbuf, vsem, c, slot))
        madd = jnp.where(sc_ref[:, c * ck:(c + 1) * ck] >= thr, 0.0, NEG_INF)
        s = lax.dot_general(qbd, kbuf[slot].astype(BF16), (((1,), (1,)), ((), ())),
                            preferred_element_type=F32)
        s = s + jnp.tile(madd, (N_HEADS, 1))
        if (c + 1) * ck + MAX_DISTANCE > past:
            s = biased(s, c * ck)
        state = softmax_step(state, s, vbuf[slot].astype(BF16))
    madd = jnp.where(sc_ref[:, past:past + LANES] >= thr, 0.0, NEG_INF)
    s = lax.dot_general(qbd, kn_ref[...], (((1,), (1,)), ((), ())), preferred_element_type=F32)
    s = biased(s + jnp.tile(madd, (N_HEADS, 1)), past)
    _, l_fin, acc = softmax_step(state, s, vn_ref[...])
    res = jnp.where(own, acc / jnp.sum(l_fin, axis=-1, keepdims=True), 0.0)
    out = res[0:sn]
    for h in range(1, N_HEADS):
        out = out + res[h * sn:(h + 1) * sn]
    o_ref[...] = out


def _sattn_call(page_table, rel_bias, q, qis, w, kin, kn, vn, cache_kidx, cache_k, cache_v, cp):
    db, sn, d_attn = q.shape
    n_pages = page_table.shape[1]
    page = cache_k.shape[1]
    assert n_pages % cp == 0 and sn == 8 and cp * page >= 2 * LANES and MAX_DISTANCE <= cp * page
    ck = cp * page
    rows = N_HEADS * sn
    n_index_bits = int(math.ceil(math.log2(n_pages * page + ck))) + 1
    per_b = lambda r, c: pl.BlockSpec((None, r, c), lambda b, pt: (b, 0, 0))
    grid_spec = pltpu.PrefetchScalarGridSpec(
        num_scalar_prefetch=1,
        grid=(db,),
        in_specs=[pl.BlockSpec(memory_space=pltpu.SMEM),
                  per_b(sn, d_attn), per_b(rows, IDX_DIM), per_b(rows, 1),
                  per_b(LANES, IDX_DIM), per_b(LANES, d_attn), per_b(LANES, d_attn),
                  pl.BlockSpec(memory_space=pl.ANY), pl.BlockSpec(memory_space=pl.ANY),
                  pl.BlockSpec(memory_space=pl.ANY)],
        out_specs=per_b(sn, d_attn),
        scratch_shapes=[
            pltpu.VMEM((sn, n_pages * page + ck), F32),
            pltpu.VMEM((2, ck, IDX_DIM), F32),
            pltpu.VMEM((2, ck, d_attn), F32),
            pltpu.VMEM((2, ck, d_attn), F32),
            pltpu.SemaphoreType.DMA((2,)), pltpu.SemaphoreType.DMA((2,)), pltpu.SemaphoreType.DMA((2,)),
        ],
    )
    return pl.pallas_call(
        functools.partial(_sattn_kernel, cp=cp, page=page, n_bisect=16, n_index_bits=n_index_bits),
        grid_spec=grid_spec,
        out_shape=jax.ShapeDtypeStruct((db, sn, d_attn), F32),
        compiler_params=_cparams(1),
        name="sattn",
    )(page_table, rel_bias, q, qis, w, kin, kn, vn, cache_kidx, cache_k, cache_v)


def _gelu_tanh(x):
    return 0.5 * x * (1.0 + jnp.tanh(math.sqrt(2.0 / math.pi) * (x + 0.044715 * x * x * x)))


def _ssm_kernel(u_ref, h0r_ref, h0i_ref, lr_ref, li_ref, ldt_ref, bre_ref, bim_ref, cre_ref, cim_ref,
                d_ref, wglu_ref, o_ref, hr_out_ref, hi_out_ref,
                lbr_ref, lbi_ref, bbr_ref, bbi_ref, bur_ref, bui_ref, hre_ref, him_ref, sr_ref, si_ref,
                *, nb, unroll):
    step = pl.program_id(0)
    tt = u_ref.shape[0]
    d_ssm = u_ref.shape[1]

    @pl.when(step == 0)
    def _():
        lr, li = lr_ref[...], li_ref[...]
        dt = jnp.exp(ldt_ref[...])
        mag = jnp.exp(lr * dt)
        lbr, lbi = mag * jnp.cos(li * dt), mag * jnp.sin(li * dt)
        den = lr * lr + li * li
        xr, xi = lbr - 1.0, lbi
        cr = (xr * lr + xi * li) / den
        ci = (xi * lr - xr * li) / den
        lbr_ref[...] = lbr
        lbi_ref[...] = lbi
        bbr_ref[...] = (cr * bre_ref[...] - ci * bim_ref[...]).astype(BF16)
        bbi_ref[...] = (cr * bim_ref[...] + ci * bre_ref[...]).astype(BF16)
        sr_ref[...] = h0r_ref[...]
        si_ref[...] = h0i_ref[...]

    u = u_ref[...]
    u16 = u.astype(BF16)
    bur_ref[...] = jnp.dot(u16, bbr_ref[...], preferred_element_type=F32)
    bui_ref[...] = jnp.dot(u16, bbi_ref[...], preferred_element_type=F32)
    lbr = jnp.broadcast_to(lbr_ref[...], (nb, lbr_ref.shape[1]))
    lbi = jnp.broadcast_to(lbi_ref[...], (nb, lbi_ref.shape[1]))

    def scan(t, carry):
        hr, hi = carry
        r0 = pl.multiple_of(t * nb, nb)
        nr = lbr * hr - lbi * hi + bur_ref[pl.ds(r0, nb), :]
        ni = lbr * hi + lbi * hr + bui_ref[pl.ds(r0, nb), :]
        hre_ref[pl.ds(r0, nb), :] = nr
        him_ref[pl.ds(r0, nb), :] = ni
        return nr, ni

    hr, hi = lax.fori_loop(0, tt // nb, scan, (sr_ref[...], si_ref[...]), unroll=unroll)
    sr_ref[...] = hr
    si_ref[...] = hi
    hr_out_ref[...] = hr
    hi_out_ref[...] = hi

    y = (jnp.dot(hre_ref[...].astype(BF16), cre_ref[...], preferred_element_type=F32)
         - jnp.dot(him_ref[...].astype(BF16), cim_ref[...], preferred_element_type=F32)
         + d_ref[...] * u)
    z = _gelu_tanh(y)
    g = jnp.dot(z.astype(BF16), wglu_ref[...], preferred_element_type=F32)
    o_ref[...] = g[:, :d_ssm] / (1.0 + jnp.exp(-g[:, d_ssm:]))


def _ssm_call(u, h0r, h0i, lr, li, ldt, bre, bim, cre, cim, d, wglu, nb, tt):
    n, d_ssm = u.shape
    ns = lr.shape[1]
    assert n % tt == 0 and tt % nb == 0
    unroll = 8 if (tt // nb) % 8 == 0 else 1
    state = jax.ShapeDtypeStruct((nb, ns), F32)
    return pl.pallas_call(
        functools.partial(_ssm_kernel, nb=nb, unroll=unroll),
        grid=(n // tt,),
        in_specs=[pl.BlockSpec((tt, d_ssm), lambda i: (i, 0)),
                  _resident((nb, ns)), _resident((nb, ns)),
                  _resident((1, ns)), _resident((1, ns)), _resident((1, ns)),
                  _resident((d_ssm, ns)), _resident((d_ssm, ns)),
                  _resident((ns, d_ssm)), _resident((ns, d_ssm)),
                  _resident((1, d_ssm)), _resident((d_ssm, 2 * d_ssm))],
        out_specs=(pl.BlockSpec((tt, d_ssm), lambda i: (i, 0)),
                   pl.BlockSpec((nb, ns), lambda i: (0, 0)), pl.BlockSpec((nb, ns), lambda i: (0, 0))),
        out_shape=(jax.ShapeDtypeStruct((n, d_ssm), F32), state, state),
        scratch_shapes=[
            pltpu.VMEM((1, ns), F32), pltpu.VMEM((1, ns), F32),
            pltpu.VMEM((d_ssm, ns), BF16), pltpu.VMEM((d_ssm, ns), BF16),
            pltpu.VMEM((tt, ns), F32), pltpu.VMEM((tt, ns), F32),
            pltpu.VMEM((tt, ns), F32), pltpu.VMEM((tt, ns), F32),
            pltpu.VMEM((nb, ns), F32), pltpu.VMEM((nb, ns), F32),
        ],
        compiler_params=_cparams(1),
        name="ssm",
    )(u, h0r, h0i, lr, li, ldt, bre, bim, cre, cim, d, wglu)


def _finish_kernel(x_ref, attn_ref, ssm_ref, gi_ref, bi_ref, wo_ref, g1_ref, b1_ref, wr_ref, br_ref,
                   wg_ref, wu_ref, wd_ref, g2_ref, b2_ref, o_ref, *, alpha, pre_ln):
    x = x_ref[...]
    if pre_ln:
        x = _layer_norm(x, gi_ref[...], bi_ref[...])
    d_attn = attn_ref.shape[1]
    mix = (jnp.dot(attn_ref[...].astype(BF16), wo_ref[:d_attn, :], preferred_element_type=F32)
           + jnp.dot(ssm_ref[...].astype(BF16), wo_ref[d_attn:, :], preferred_element_type=F32))
    x1 = _layer_norm(alpha * x + mix, g1_ref[...], b1_ref[...])
    x16 = x1.astype(BF16)

    lg = jnp.dot(x16, wr_ref[...], preferred_element_type=F32) + br_ref[...]
    lane = lax.broadcasted_iota(jnp.int32, lg.shape, 1)
    big = jnp.int32(1 << 20)
    is_grp = lane < N_EXPERT_GROUPS
    gl = jnp.where(is_grp, lg, NEG_INF)
    gmax = jnp.max(gl, axis=-1, keepdims=True)
    g_sel = jnp.min(jnp.where(gl == gmax, lane, big), axis=-1, keepdims=True)
    p_grp = 1.0 / jnp.sum(jnp.where(is_grp, jnp.exp(gl - gmax), 0.0), axis=-1, keepdims=True)
    e_idx = lane - N_EXPERT_GROUPS
    in_sel = jnp.logical_and(jnp.logical_and(e_idx >= 0, e_idx < N_EXPERTS),
                             e_idx // EXPERTS_PER_GROUP == g_sel)
    el = jnp.where(in_sel, lg, NEG_INF)
    v1 = jnp.max(el, axis=-1, keepdims=True)
    i1 = jnp.min(jnp.where(el == v1, lane, big), axis=-1, keepdims=True)
    el2 = jnp.where(lane == i1, NEG_INF, el)
    v2 = jnp.max(el2, axis=-1, keepdims=True)
    i2 = jnp.min(jnp.where(el2 == v2, lane, big), axis=-1, keepdims=True)
    e21 = jnp.exp(v2 - v1)
    w1 = p_grp / (1.0 + e21)
    w2 = p_grp * e21 / (1.0 + e21)
    gates = jnp.where(lane == i1, w1, jnp.where(lane == i2, w2, 0.0))

    y = jnp.zeros(x1.shape, F32)
    for e in range(N_EXPERTS):
        gate = gates[:, N_EXPERT_GROUPS + e:N_EXPERT_GROUPS + e + 1]
        hg = jnp.dot(x16, wg_ref[e], preferred_element_type=F32)
        hu = jnp.dot(x16, wu_ref[e], preferred_element_type=F32)
        hid = hg / (1.0 + jnp.exp(-hg)) * hu * gate
        y = y + jnp.dot(hid.astype(BF16), wd_ref[e], preferred_element_type=F32)
    o_ref[...] = _layer_norm(alpha * x1 + y, g2_ref[...], b2_ref[...])


def _finish_call(x, attn, ssm, gi, bi, wo, g1, b1, wr, br, wg, wu, wd, g2, b2, alpha, pre_ln, tm):
    n, d = x.shape
    row = lambda c: pl.BlockSpec((tm, c), lambda i: (i, 0))
    return pl.pallas_call(
        functools.partial(_finish_kernel, alpha=alpha, pre_ln=pre_ln),
        grid=(n // tm,),
        in_specs=[row(d), row(attn.shape[1]), row(ssm.shape[1]),
                  _resident(gi.shape), _resident(bi.shape), _resident(wo.shape),
                  _resident(g1.shape), _resident(b1.shape), _resident(wr.shape), _resident(br.shape),
                  _resident(wg.shape), _resident(wu.shape), _resident(wd.shape),
                  _resident(g2.shape), _resident(b2.shape)],
        out_specs=row(d),
        out_shape=jax.ShapeDtypeStruct((n, d), F32),
        compiler_params=_cparams(1),
        name="finish",
    )(x, attn, ssm, gi, bi, wo, g1, b1, wr, br, wg, wu, wd, g2, b2)


def _row(a):
    return a.reshape(1, -1).astype(F32)


def _block_diag(a):
    g, r, c = a.shape
    return jnp.einsum("grc,gh->grhc", a, jnp.eye(g, dtype=a.dtype)).reshape(g * r, g * c)


def _pad_rows(a, rows):
    return jnp.pad(a, ((0, 0), (0, rows - a.shape[1]), (0, 0)))


def _pick_tile(n, pref):
    t = min(n, pref)
    while n % t:
        t //= 2
    return t


def kernel(x_prompt, x_sample, cache_k, cache_v, cache_kidx, state_ssm_re, state_ssm_im, page_table, ln_in_g, ln_in_b, rel_bias, w_in, idx_ln_g, idx_ln_b, ssm_lam_re, ssm_lam_im, ssm_log_dt, ssm_b_re, ssm_b_im, ssm_c_re, ssm_c_im, ssm_d, w_glu, w_out, ln_mix_g, ln_mix_b, w_router_grp, b_router_grp, w_router_exp, b_router_exp, w_gate, w_up, w_down, ln_ffn_g, ln_ffn_b):
    bsz, seq, d_model = x_prompt.shape
    dbsz, dseq, _ = x_sample.shape
    depth = w_in.shape[0]
    d_attn = N_HEADS * HEAD_DIM
    n_grp = ssm_lam_re.shape[1]
    d_ssm = n_grp * SSM_GROUP
    n_state = n_grp * SSM_STATE
    n_pool, page = cache_k.shape[1], cache_k.shape[2]
    alpha = (2.0 * depth) ** 0.25
    sizes = (d_attn, d_attn, d_attn, N_IDX_HEADS * IDX_DIM, IDX_DIM, N_IDX_HEADS, d_ssm)
    cuts = [0] + [int(c) for c in np.cumsum(sizes)]
    rel_bias = rel_bias.astype(F32)
    page_table = page_table.astype(jnp.int32)

    xp = x_prompt.reshape(bsz * seq, d_model)
    xs = x_sample.reshape(dbsz * dseq, d_model)
    outs = [[] for _ in range(10)]
    for l in range(depth):
        pre_ln = l == 0
        seg = [w_in[l][:, cuts[j]:cuts[j + 1]] for j in range(7)]
        pad = jnp.zeros((d_model, LANES - IDX_DIM - N_IDX_HEADS), w_in.dtype)
        w_all = jnp.concatenate([seg[0], seg[1], seg[2], seg[3], seg[6], seg[4], seg[5], pad], axis=1).astype(BF16)
        ig = jnp.pad(idx_ln_g[l], (0, LANES - IDX_DIM)).reshape(1, LANES).astype(F32)
        ib = jnp.pad(idx_ln_b[l], (0, LANES - IDX_DIM)).reshape(1, LANES).astype(F32)
        ssm_w = (_row(ssm_lam_re[l]), _row(ssm_lam_im[l]), _row(jnp.repeat(ssm_log_dt[l], SSM_STATE)),
                 _block_diag(jnp.swapaxes(ssm_b_re[l], 1, 2)).astype(F32),
                 _block_diag(jnp.swapaxes(ssm_b_im[l], 1, 2)).astype(F32),
                 _block_diag(jnp.swapaxes(ssm_c_re[l], 1, 2)).astype(BF16),
                 _block_diag(jnp.swapaxes(ssm_c_im[l], 1, 2)).astype(BF16),
                 _row(ssm_d[l]), w_glu[l].astype(BF16))
        wr = jnp.concatenate([w_router_grp[l], jnp.transpose(w_router_exp[l], (1, 0, 2)).reshape(d_model, N_EXPERTS)], axis=1)
        wr = jnp.pad(wr, ((0, 0), (0, LANES - wr.shape[1]))).astype(BF16)
        br = jnp.pad(jnp.concatenate([b_router_grp[l], b_router_exp[l].reshape(-1)]),
                     (0, LANES - N_EXPERT_GROUPS - N_EXPERTS)).reshape(1, LANES).astype(F32)
        fin_w = (_row(ln_in_g), _row(ln_in_b), w_out[l].astype(BF16), _row(ln_mix_g[l]), _row(ln_mix_b[l]), wr, br,
                 w_gate[l].astype(BF16), w_up[l].astype(BF16), w_down[l].astype(BF16),
                 _row(ln_ffn_g[l]), _row(ln_ffn_b[l]))

        q, k, v, kb, vb, qi, u, misc = _proj_call(xp, _row(ln_in_g), _row(ln_in_b), w_all, ig, ib,
                                                  d_attn, d_ssm, pre_ln, _pick_tile(bsz * seq, 512))
        r3 = lambda a: a.reshape(bsz, seq, a.shape[-1])
        kit = jnp.swapaxes(r3(misc)[..., :IDX_DIM].astype(BF16), 1, 2)
        kit = jnp.concatenate([kit, kit], axis=1)
        attn = _pattn_call(rel_bias, r3(q), r3(qi), r3(misc), kit, r3(kb), r3(vb), tq=LANES, kb=512)
        u_tm = jnp.swapaxes(r3(u), 0, 1).reshape(seq * bsz, d_ssm)
        h0 = jnp.zeros((bsz, n_state), F32)
        ssm_tm, hr, hi = _ssm_call(u_tm, h0, h0, *ssm_w, nb=bsz, tt=_pick_tile(seq, 512) * bsz)
        ssm_o = jnp.swapaxes(ssm_tm.reshape(seq, bsz, d_ssm), 0, 1).reshape(bsz * seq, d_ssm)
        xp = _finish_call(xp, attn.reshape(bsz * seq, d_attn), ssm_o, *fin_w, alpha=alpha, pre_ln=pre_ln,
                          tm=_pick_tile(bsz * seq, 256))
        outs[0].append(k.reshape(bsz, seq, N_HEADS, HEAD_DIM))
        outs[1].append(v.reshape(bsz, seq, N_HEADS, HEAD_DIM))
        outs[2].append(r3(misc)[..., :IDX_DIM])
        outs[3].append(hr.reshape(bsz, n_grp, SSM_STATE).astype(state_ssm_re.dtype))
        outs[4].append(hi.reshape(bsz, n_grp, SSM_STATE).astype(state_ssm_im.dtype))

        q, k, v, kb, vb, qi, u, misc = _proj_call(xs, _row(ln_in_g), _row(ln_in_b), w_all, ig, ib,
                                                  d_attn, d_ssm, pre_ln, _pick_tile(dbsz * dseq, 512))
        s3 = lambda a: a.reshape(dbsz, dseq, a.shape[-1])
        rows = N_IDX_HEADS * dseq
        qis = jnp.swapaxes(qi.reshape(dbsz, dseq, N_IDX_HEADS, IDX_DIM), 1, 2).reshape(dbsz, rows, IDX_DIM)
        wis = jnp.swapaxes(s3(misc)[..., IDX_DIM:IDX_DIM + N_IDX_HEADS], 1, 2).reshape(dbsz, rows, 1)
        kin = _pad_rows(s3(misc)[..., :IDX_DIM].astype(BF16), LANES)
        attn = _sattn_call(page_table, rel_bias, s3(q), qis, wis, kin, _pad_rows(s3(kb), LANES), _pad_rows(s3(vb), LANES),
                           cache_kidx[l], cache_k[l].reshape(n_pool, page, d_attn), cache_v[l].reshape(n_pool, page, d_attn),
                           cp=min(16, page_table.shape[1]))
        u_tm = jnp.swapaxes(s3(u), 0, 1).reshape(dseq * dbsz, d_ssm)
        ssm_tm, hr, hi = _ssm_call(u_tm, state_ssm_re[l].reshape(dbsz, n_state).astype(F32),
                                   state_ssm_im[l].reshape(dbsz, n_state).astype(F32), *ssm_w, nb=dbsz, tt=dseq * dbsz)
        ssm_o = jnp.swapaxes(ssm_tm.reshape(dseq, dbsz, d_ssm), 0, 1).reshape(dbsz * dseq, d_ssm)
        xs = _finish_call(xs, attn.reshape(dbsz * dseq, d_attn), ssm_o, *fin_w, alpha=alpha, pre_ln=pre_ln,
                          tm=_pick_tile(dbsz * dseq, 256))
        outs[5].append(k.reshape(dbsz, dseq, N_HEADS, HEAD_DIM))
        outs[6].append(v.reshape(dbsz, dseq, N_HEADS, HEAD_DIM))
        outs[7].append(s3(misc)[..., :IDX_DIM])
        outs[8].append(hr.reshape(dbsz, n_grp, SSM_STATE).astype(state_ssm_re.dtype))
        outs[9].append(hi.reshape(dbsz, n_grp, SSM_STATE).astype(state_ssm_im.dtype))

    return (xp.reshape(bsz, seq, d_model), xs.reshape(dbsz, dseq, d_model)) + tuple(jnp.stack(o) for o in outs)
```

```python
import functools
import math

import numpy as np
import jax
import jax.numpy as jnp
from jax import lax
from jax.experimental import pallas as pl
from jax.experimental.pallas import tpu as pltpu

N_HEADS = 8
HEAD_DIM = 64
N_IDX_HEADS = 8
IDX_DIM = 64
TOPK_MAX = 256
NUM_BUCKETS = 32
MAX_DISTANCE = 128
SSM_GROUP = 16
SSM_STATE = 64
N_EXPERT_GROUPS = 4
EXPERTS_PER_GROUP = 4
N_EXPERTS = N_EXPERT_GROUPS * EXPERTS_PER_GROUP
LN_EPS = 1e-5
NEG_INF = -1e30
LOG2E = math.log2(math.e)

LANES = 128
VMEM_LIMIT = 56 * 1024 * 1024

PROJ_ROWS = 512
FINISH_ROWS = 256
SSM_ROWS = 512
QUERY_BLOCK = LANES
KEY_CHUNK = 512
PAGES_PER_CHUNK = 16
N_BISECT = 16

F32 = jnp.float32
BF16 = jnp.bfloat16
_NT = (((1,), (1,)), ((), ()))


def _cparams(n_grid, vmem=VMEM_LIMIT):
    return pltpu.CompilerParams(dimension_semantics=("arbitrary",) * n_grid, vmem_limit_bytes=vmem)


def _resident(shape):
    nd = len(shape)
    return pl.BlockSpec(shape, lambda *_: (0,) * nd, pipeline_mode=pl.Buffered(1))


def _layer_norm(x, g, b):
    mu = jnp.mean(x, axis=-1, keepdims=True)
    xc = x - mu
    var = jnp.mean(xc * xc, axis=-1, keepdims=True)
    return xc * lax.rsqrt(var + LN_EPS) * g + b


def _proj_kernel(x_ref, g_ref, b_ref, wr_ref, wt_ref, ig_ref, ib_ref,
                 q_ref, qi_ref, u_ref, ws_ref, kt_ref, vt_ref, ktb_ref, vtb_ref, kit_ref, kitb_ref,
                 *, d_attn, d_ssm, pre_ln):
    xn = x_ref[...]
    if pre_ln:
        xn = _layer_norm(xn, g_ref[...], b_ref[...])
    x16 = xn.astype(BF16)
    d_qi = N_IDX_HEADS * IDX_DIM
    hr = jnp.dot(x16, wr_ref[...], preferred_element_type=F32)
    ht = lax.dot_general(wt_ref[...], x16, _NT, preferred_element_type=F32)
    q_ref[...] = (hr[:, :d_attn] * (HEAD_DIM ** -0.5 * LOG2E)).astype(BF16)
    qi_ref[...] = (hr[:, d_attn:d_attn + d_qi] * IDX_DIM ** -0.5).astype(BF16)
    u_ref[...] = hr[:, d_attn + d_qi:d_attn + d_qi + d_ssm]
    ws_ref[...] = hr[:, d_attn + d_qi + d_ssm:] * N_IDX_HEADS ** -0.5
    kt = ht[:d_attn]
    vt = ht[d_attn:2 * d_attn]
    kt_ref[...] = kt
    vt_ref[...] = vt
    ktb_ref[...] = kt.astype(BF16)
    vtb_ref[...] = vt.astype(BF16)
    ki = ht[2 * d_attn:]
    mu = jnp.mean(ki, axis=0, keepdims=True)
    kc = ki - mu
    var = jnp.mean(kc * kc, axis=0, keepdims=True)
    kin = kc * lax.rsqrt(var + LN_EPS) * ig_ref[...] + ib_ref[...]
    kit_ref[...] = kin
    kin16 = kin.astype(BF16)
    kitb_ref[...] = jnp.concatenate([kin16, kin16], axis=0)


def _proj_call(x, ln_g, ln_b, w_row, w_t, ig, ib, d_attn, d_ssm, pre_ln, tm):
    n, d = x.shape
    d_qi = N_IDX_HEADS * IDX_DIM
    row = lambda c, dt: (jax.ShapeDtypeStruct((n, c), dt), pl.BlockSpec((tm, c), lambda i: (i, 0)))
    col = lambda r, dt: (jax.ShapeDtypeStruct((r, n), dt), pl.BlockSpec((r, tm), lambda i: (0, i)))
    outs = [row(d_attn, BF16), row(d_qi, BF16), row(d_ssm, F32), row(LANES, F32),
            col(d_attn, F32), col(d_attn, F32), col(d_attn, BF16), col(d_attn, BF16),
            col(IDX_DIM, F32), col(2 * IDX_DIM, BF16)]
    return pl.pallas_call(
        functools.partial(_proj_kernel, d_attn=d_attn, d_ssm=d_ssm, pre_ln=pre_ln),
        grid=(n // tm,),
        in_specs=[pl.BlockSpec((tm, d), lambda i: (i, 0)), _resident((1, d)), _resident((1, d)),
                  _resident(w_row.shape), _resident(w_t.shape), _resident((IDX_DIM, 1)), _resident((IDX_DIM, 1))],
        out_specs=tuple(o[1] for o in outs),
        out_shape=tuple(o[0] for o in outs),
        compiler_params=_cparams(1),
        name="proj",
    )(x, ln_g, ln_b, w_row, w_t, ig, ib)


def _rel_bucket(dist):
    n = jnp.maximum(dist, 0)
    max_exact = NUM_BUCKETS // 2
    scaled = jnp.log(jnp.maximum(n, 1).astype(F32) / max_exact) / math.log(MAX_DISTANCE / max_exact)
    large = jnp.minimum(max_exact + (scaled * (NUM_BUCKETS - max_exact)).astype(jnp.int32), NUM_BUCKETS - 1)
    return jnp.where(n < max_exact, n, large)


def _shifted_bias(dist, rb_ref, h):
    bkt = _rel_bucket(dist)
    far = rb_ref[NUM_BUCKETS - 1, h]
    out = jnp.zeros(dist.shape, F32)
    for b in range(NUM_BUCKETS - 1):
        out = jnp.where(bkt == b, rb_ref[b, h] - far, out)
    return out * LOG2E


def _lane_tile_sum(x):
    acc = x[:, :LANES]
    for t in range(1, x.shape[1] // LANES):
        acc = acc + x[:, t * LANES:(t + 1) * LANES]
    return acc


def _lane_tile_max(x):
    acc = x[:, :LANES]
    for t in range(1, x.shape[1] // LANES):
        acc = jnp.maximum(acc, x[:, t * LANES:(t + 1) * LANES])
    return acc


def _lane_tile_min(x):
    acc = x[:, :LANES]
    for t in range(1, x.shape[1] // LANES):
        acc = jnp.minimum(acc, x[:, t * LANES:(t + 1) * LANES])
    return acc


def _topk_threshold(sc_ref, nch, kb, k_row, rmin, rmax, n_bisect, n_index_bits):
    rows = sc_ref.shape[0]

    def chunk(c):
        return sc_ref[:, pl.ds(pl.multiple_of(c * kb, kb), kb)]

    def count_ge(t):
        def body(c, cnt):
            return cnt + _lane_tile_sum(jnp.where(chunk(c) >= t, 1.0, 0.0))
        cnt = lax.fori_loop(0, nch, body, jnp.zeros((rows, LANES), F32))
        return jnp.sum(cnt, axis=-1, keepdims=True)

    def bisect(_, lohi):
        lo, hi = lohi
        mid = 0.5 * (lo + hi)
        ok = count_ge(mid) >= k_row
        return jnp.where(ok, mid, lo), jnp.where(ok, hi, mid)

    _, hi = lax.fori_loop(0, n_bisect, bisect, (rmin, rmax))

    def count_ge_and_next(m):
        def body(c, carry):
            cnt, nxt = carry
            x = chunk(c)
            cnt = cnt + _lane_tile_sum(jnp.where(x >= m, 1.0, 0.0))
            nxt = jnp.maximum(nxt, _lane_tile_max(jnp.where(x < m, x, NEG_INF)))
            return cnt, nxt
        cnt, nxt = lax.fori_loop(0, nch, body, (jnp.zeros((rows, LANES), F32),
                                                jnp.full((rows, LANES), NEG_INF, F32)))
        return jnp.sum(cnt, axis=-1, keepdims=True), jnp.max(nxt, axis=-1, keepdims=True)

    def first_le(t):
        def body(c, nxt):
            x = chunk(c)
            return jnp.maximum(nxt, _lane_tile_max(jnp.where(x <= t, x, NEG_INF)))
        nxt = lax.fori_loop(0, nch, body, jnp.full((rows, LANES), NEG_INF, F32))
        return jnp.max(nxt, axis=-1, keepdims=True)

    def walk_cond(st):
        return st[3] > 0.0

    def walk_body(st):
        m, cfin, done, _ = st
        cnt, nxt = count_ge_and_next(m)
        ok = jnp.logical_and(cnt >= k_row, done < 0.5)
        cfin = jnp.where(ok, cnt, cfin)
        done = jnp.where(cnt >= k_row, 1.0, done)
        m = jnp.where(done > 0.5, m, nxt)
        return m, cfin, done, jnp.sum(1.0 - done)

    zeros = jnp.zeros((rows, 1), F32)
    thr, cfin, _, _ = lax.while_loop(walk_cond, walk_body, (first_le(hi), zeros, zeros, jnp.float32(rows)))

    @pl.when(jnp.sum(jnp.where(cfin > k_row, 1.0, 0.0)) > 0.0)
    def _():
        def count_gt():
            def body(c, cnt):
                return cnt + _lane_tile_sum(jnp.where(chunk(c) > thr, 1.0, 0.0))
            cnt = lax.fori_loop(0, nch, body, jnp.zeros((rows, LANES), F32))
            return jnp.sum(cnt, axis=-1, keepdims=True)

        need = k_row - count_gt()

        def count_eq_upto(j):
            def body(c, cnt):
                x = chunk(c)
                idx = (c * kb + lax.broadcasted_iota(jnp.int32, x.shape, 1)).astype(F32)
                hit = jnp.where(x == thr, jnp.where(idx <= j, 1.0, 0.0), 0.0)
                return cnt + _lane_tile_sum(hit)
            cnt = lax.fori_loop(0, nch, body, jnp.zeros((rows, LANES), F32))
            return jnp.sum(cnt, axis=-1, keepdims=True)

        def ibisect(_, lohi):
            lo, hi_i = lohi
            mid = jnp.floor(0.5 * (lo + hi_i))
            ok = count_eq_upto(mid) >= need
            return jnp.where(ok, lo, mid), jnp.where(ok, mid, hi_i)

        width = jnp.float32(1.0) * (nch * kb)
        _, cut = lax.fori_loop(0, n_index_bits, ibisect,
                               (jnp.full((rows, 1), -1.0, F32), jnp.zeros((rows, 1), F32) + (width - 1.0)))

        def drop(c, _):
            off = pl.multiple_of(c * kb, kb)
            x = sc_ref[:, pl.ds(off, kb)]
            idx = (c * kb + lax.broadcasted_iota(jnp.int32, x.shape, 1)).astype(F32)
            dropped = jnp.where(x == thr, jnp.where(idx > cut, 1.0, 0.0), 0.0)
            sc_ref[:, pl.ds(off, kb)] = jnp.where(dropped > 0.5, NEG_INF, x)
            return 0
        lax.fori_loop(0, nch, drop, 0)

    return thr


def _pattn_kernel(rb_ref, q_ref, qi_ref, ws_ref, kit_ref, kt_ref, vt_ref, o_ref,
                  sc_ref, qm_ref, qim_ref, bias_ref, m_ref, l_ref, acc_ref, *, tq, kb, n_bisect, n_index_bits):
    i = pl.program_id(1)
    n_pairs = N_HEADS // 2
    nch = ((i + 1) * tq + kb - 1) // kb
    sub = kb // tq
    lane = lax.broadcasted_iota(jnp.int32, (tq, LANES), 1)
    lo_half = lane < HEAD_DIM
    q_pos = i * tq + lax.broadcasted_iota(jnp.int32, (tq, 1), 0)

    @pl.when(i == 0)
    def _():
        r = lax.broadcasted_iota(jnp.int32, (tq, tq), 0)
        c = lax.broadcasted_iota(jnp.int32, (tq, tq), 1)
        for h in range(N_HEADS):
            bias_ref[0, h] = jnp.zeros((tq, tq), F32)
            bias_ref[1, h] = _shifted_bias(r - c + tq, rb_ref, h)
            bias_ref[2, h] = _shifted_bias(r - c, rb_ref, h)

    for h in range(N_HEADS):
        keep = lo_half if h % 2 == 0 else jnp.logical_not(lo_half)
        pr = h // 2
        qm_ref[h] = jnp.where(keep, q_ref[:, pr * LANES:(pr + 1) * LANES].astype(F32), 0.0).astype(BF16)
        qim_ref[h] = jnp.where(keep, qi_ref[:, pr * LANES:(pr + 1) * LANES].astype(F32), 0.0).astype(BF16)

    wcols = [ws_ref[:, h:h + 1] for h in range(N_IDX_HEADS)]

    def score_chunk(c, carry):
        rmin, rmax = carry
        off = pl.multiple_of(c * kb, kb)
        kt = kit_ref[:, pl.ds(off, kb)]
        acc = jnp.zeros((tq, kb), F32)
        for h in range(N_IDX_HEADS):
            s = jnp.dot(qim_ref[h], kt, preferred_element_type=F32)
            acc = acc + jnp.maximum(s, 0.0) * wcols[h]
        key_pos = off + lax.broadcasted_iota(jnp.int32, (tq, kb), 1)
        valid = key_pos <= q_pos
        sc_ref[:, pl.ds(off, kb)] = jnp.where(valid, acc, NEG_INF)
        rmax = jnp.maximum(rmax, _lane_tile_max(jnp.where(valid, acc, NEG_INF)))
        rmin = jnp.minimum(rmin, _lane_tile_min(jnp.where(valid, acc, -NEG_INF)))
        return rmin, rmax

    rmin, rmax = lax.fori_loop(0, nch, score_chunk,
                               (jnp.full((tq, LANES), -NEG_INF, F32), jnp.full((tq, LANES), NEG_INF, F32)))
    rmin = jnp.min(rmin, axis=-1, keepdims=True)
    rmax = jnp.max(rmax, axis=-1, keepdims=True)

    topk = min(TOPK_MAX, kit_ref.shape[1] // 4)
    k_row = jnp.minimum(q_pos + 1, topk).astype(F32)
    thr = _topk_threshold(sc_ref, nch, kb, k_row, rmin, rmax, n_bisect, n_index_bits)

    def to_mask(c, _):
        off = pl.multiple_of(c * kb, kb)
        sc_ref[:, pl.ds(off, kb)] = jnp.where(sc_ref[:, pl.ds(off, kb)] >= thr, 0.0, NEG_INF)
        return 0
    lax.fori_loop(0, nch, to_mask, 0)

    def logits(c, h, off, with_bias):
        pr = h // 2
        kp = kt_ref[pr * LANES:(pr + 1) * LANES, pl.ds(off, kb)]
        s = jnp.dot(qm_ref[h], kp, preferred_element_type=F32) + sc_ref[:, pl.ds(off, kb)]
        if with_bias:
            tiles = [bias_ref[jnp.clip(c * sub + t - i + 2, 0, 2), h] for t in range(sub)]
            s = s + jnp.concatenate(tiles, axis=1)
        return s

    def max_chunk(c, with_bias):
        off = pl.multiple_of(c * kb, kb)
        for h in range(N_HEADS):
            m_ref[h] = jnp.maximum(m_ref[h], _lane_tile_max(logits(c, h, off, with_bias)))

    def acc_chunk(c, with_bias):
        off = pl.multiple_of(c * kb, kb)
        for pr in range(n_pairs):
            vp = vt_ref[pr * LANES:(pr + 1) * LANES, pl.ds(off, kb)]
            for hh in range(2):
                h = 2 * pr + hh
                p = jnp.exp2(logits(c, h, off, with_bias) - pltpu.repeat(m_ref[h], sub, axis=1))
                l_ref[h] = l_ref[h] + _lane_tile_sum(p)
                pv = lax.dot_general(p.astype(BF16), vp, _NT, preferred_element_type=F32)
                mine = lo_half if hh == 0 else jnp.logical_not(lo_half)
                acc_ref[pr] = acc_ref[pr] + jnp.where(mine, pv, 0.0)

    def sweep(fn):
        n_far = jnp.maximum((i * tq - MAX_DISTANCE + 1) // kb, 0)

        def far_body(c, _):
            fn(c, False)
            return 0

        def near_body(c, _):
            fn(c, True)
            return 0

        lax.fori_loop(0, n_far, far_body, 0)
        lax.fori_loop(n_far, nch, near_body, 0)

    m_ref[...] = jnp.full(m_ref.shape, NEG_INF, F32)
    sweep(max_chunk)
    for h in range(N_HEADS):
        m_ref[h] = jnp.broadcast_to(jnp.max(m_ref[h], axis=-1, keepdims=True), (tq, LANES))
    l_ref[...] = jnp.zeros(l_ref.shape, F32)
    acc_ref[...] = jnp.zeros(acc_ref.shape, F32)
    sweep(acc_chunk)

    for pr in range(n_pairs):
        l0 = jnp.sum(l_ref[2 * pr], axis=-1, keepdims=True)
        l1 = jnp.sum(l_ref[2 * pr + 1], axis=-1, keepdims=True)
        o_ref[:, pr * LANES:(pr + 1) * LANES] = acc_ref[pr] / jnp.where(lo_half, l0, l1)


def _pattn_call(rel_bias, q, qi, ws, kit, kt16, vt16, tq, kb):
    bsz, s, d_attn = q.shape
    assert tq == LANES and MAX_DISTANCE <= tq and s % kb == 0 and kb % tq == 0
    n_index_bits = int(math.ceil(math.log2(s))) + 1
    blk = lambda c: pl.BlockSpec((None, tq, c), lambda b, i: (b, i, 0))
    whole = lambda r, c: pl.BlockSpec((None, r, c), lambda b, i: (b, 0, 0), pipeline_mode=pl.Buffered(1))
    return pl.pallas_call(
        functools.partial(_pattn_kernel, tq=tq, kb=kb, n_bisect=N_BISECT, n_index_bits=n_index_bits),
        grid=(bsz, s // tq),
        in_specs=[pl.BlockSpec(memory_space=pltpu.SMEM),
                  blk(d_attn), blk(N_IDX_HEADS * IDX_DIM), blk(LANES),
                  whole(2 * IDX_DIM, s), whole(d_attn, s), whole(d_attn, s)],
        out_specs=blk(d_attn),
        out_shape=jax.ShapeDtypeStruct((bsz, s, d_attn), F32),
        scratch_shapes=[
            pltpu.VMEM((tq, s), F32),
            pltpu.VMEM((N_HEADS, tq, LANES), BF16),
            pltpu.VMEM((N_IDX_HEADS, tq, LANES), BF16),
            pltpu.VMEM((3, N_HEADS, tq, tq), F32),
            pltpu.VMEM((N_HEADS, tq, LANES), F32),
            pltpu.VMEM((N_HEADS, tq, LANES), F32),
            pltpu.VMEM((N_HEADS // 2, tq, LANES), F32),
        ],
        compiler_params=_cparams(2),
        name="pattn",
    )(rel_bias, q, qi, ws, kit, kt16, vt16)


def _sattn_kernel(pt_ref, rb_ref, q_ref, qis_ref, w_ref, kin_ref, kn_ref, vn_ref,
                  cki_hbm, ck_hbm, cv_hbm, o_ref,
                  sc_ref, ibuf, kbuf, vbuf, isem, ksem, vsem, *, cp, page, n_bisect, n_index_bits):
    b = pl.program_id(0)
    n_pages = pt_ref.shape[1]
    nc = n_pages // cp
    ck = cp * page
    past = n_pages * page
    sn = q_ref.shape[0]
    rows = N_HEADS * sn
    d_attn = q_ref.shape[1]

    def page_copies(cache, buf, sem, c, slot):
        return [pltpu.make_async_copy(cache.at[pt_ref[b, c * cp + j]],
                                      buf.at[slot, :, pl.ds(j * page, page)], sem.at[slot])
                for j in range(cp)]

    def start(copies):
        for cpy in copies:
            cpy.start()

    def wait(copies):
        for cpy in copies:
            cpy.wait()

    start(page_copies(cki_hbm, ibuf, isem, 0, 0))
    start(page_copies(ck_hbm, kbuf, ksem, 0, 0))
    start(page_copies(cv_hbm, vbuf, vsem, 0, 0))

    q_pos = past + lax.broadcasted_iota(jnp.int32, (sn, 1), 0)
    qis = qis_ref[...]
    w = w_ref[...]

    def head_sum(s):
        acc = jnp.zeros((sn, s.shape[1]), F32)
        for h in range(N_IDX_HEADS):
            acc = acc + jnp.maximum(s[h * sn:(h + 1) * sn], 0.0) * w[h * sn:(h + 1) * sn]
        return acc

    rmin = jnp.full((sn, LANES), -NEG_INF, F32)
    rmax = jnp.full((sn, LANES), NEG_INF, F32)
    for c in range(nc):
        slot = c % 2
        if c + 1 < nc:
            start(page_copies(cki_hbm, ibuf, isem, c + 1, 1 - slot))
        wait(page_copies(cki_hbm, ibuf, isem, c, slot))
        acc = head_sum(jnp.dot(qis, ibuf[slot].astype(BF16), preferred_element_type=F32))
        sc_ref[:, c * ck:(c + 1) * ck] = acc
        rmax = jnp.maximum(rmax, _lane_tile_max(acc))
        rmin = jnp.minimum(rmin, _lane_tile_min(acc))
    acc = head_sum(jnp.dot(qis, kin_ref[...], preferred_element_type=F32))
    lane = lax.broadcasted_iota(jnp.int32, (sn, LANES), 1)
    valid = jnp.logical_and(lane < sn, past + lane <= q_pos)
    sc_ref[:, past:past + LANES] = jnp.where(valid, acc, NEG_INF)
    sc_ref[:, past + LANES:past + ck] = jnp.full((sn, ck - LANES), NEG_INF, F32)
    rmax = jnp.max(jnp.maximum(rmax, jnp.where(valid, acc, NEG_INF)), axis=-1, keepdims=True)
    rmin = jnp.min(jnp.minimum(rmin, jnp.where(valid, acc, -NEG_INF)), axis=-1, keepdims=True)

    topk = min(TOPK_MAX, (past + sn) // 4)
    k_row = jnp.minimum(q_pos + 1, topk).astype(F32)
    thr = _topk_threshold(sc_ref, nc + 1, ck, k_row, rmin, rmax, n_bisect, n_index_bits)

    r_head = lax.broadcasted_iota(jnp.int32, (rows, d_attn), 0) // sn
    c_head = lax.broadcasted_iota(jnp.int32, (rows, d_attn), 1) // HEAD_DIM
    own = r_head == c_head
    qbd = jnp.where(own, jnp.tile(q_ref[...].astype(F32), (N_HEADS, 1)), 0.0).astype(BF16)

    def softmax_step(state, s, vals_t):
        m_prev, l_prev, acc_prev = state
        m_next = jnp.maximum(m_prev, jnp.max(s, axis=-1, keepdims=True))
        alpha = jnp.exp2(m_prev - m_next)
        p = jnp.exp2(s - m_next[:, :1])
        l_next = alpha * l_prev + _lane_tile_sum(p)
        pv = lax.dot_general(p.astype(BF16), vals_t, _NT, preferred_element_type=F32)
        return m_next, l_next, acc_prev * alpha[:, :1] + pv

    def biased(s, key0):
        dist = q_pos - (key0 + lax.broadcasted_iota(jnp.int32, (sn, s.shape[1]), 1))
        return s + jnp.concatenate([_shifted_bias(dist, rb_ref, h) for h in range(N_HEADS)], axis=0)

    state = (jnp.full((rows, LANES), NEG_INF, F32), jnp.zeros((rows, LANES), F32),
             jnp.zeros((rows, d_attn), F32))
    for c in range(nc):
        slot = c % 2
        if c + 1 < nc:
            start(page_copies(ck_hbm, kbuf, ksem, c + 1, 1 - slot))
            start(page_copies(cv_hbm, vbuf, vsem, c + 1, 1 - slot))
        wait(page_copies(ck_hbm, kbuf, ksem, c, slot))
        wait(page_copies(cv_hbm, vbuf, vsem, c, slot))
        madd = jnp.where(sc_ref[:, c * ck:(c + 1) * ck] >= thr, 0.0, NEG_INF)
        s = jnp.dot(qbd, kbuf[slot].astype(BF16), preferred_element_type=F32)
        s = s + jnp.tile(madd, (N_HEADS, 1))
        if (c + 1) * ck + MAX_DISTANCE > past:
            s = biased(s, c * ck)
        state = softmax_step(state, s, vbuf[slot].astype(BF16))
    madd = jnp.where(sc_ref[:, past:past + LANES] >= thr, 0.0, NEG_INF)
    s = jnp.dot(qbd, kn_ref[...], preferred_element_type=F32)
    s = biased(s + jnp.tile(madd, (N_HEADS, 1)), past)
    _, l_fin, acc = softmax_step(state, s, vn_ref[...])
    res = jnp.where(own, acc / jnp.sum(l_fin, axis=-1, keepdims=True), 0.0)
    out = res[0:sn]
    for h in range(1, N_HEADS):
        out = out + res[h * sn:(h + 1) * sn]
    o_ref[...] = out


def _sattn_call(page_table, rel_bias, q, qis, w, kin_t, kn_t, vn_t, cache_kidx_t, cache_k_t, cache_v_t, cp):
    db, sn, d_attn = q.shape
    n_pages = page_table.shape[1]
    page = cache_k_t.shape[2]
    assert n_pages % cp == 0 and sn == 8 and page == LANES and cp >= 2 and MAX_DISTANCE <= cp * page
    ck = cp * page
    rows = N_HEADS * sn
    n_index_bits = int(math.ceil(math.log2(n_pages * page + ck))) + 1
    per_b = lambda r, c: pl.BlockSpec((None, r, c), lambda b, pt: (b, 0, 0))
    grid_spec = pltpu.PrefetchScalarGridSpec(
        num_scalar_prefetch=1,
        grid=(db,),
        in_specs=[pl.BlockSpec(memory_space=pltpu.SMEM),
                  per_b(sn, d_attn), per_b(rows, IDX_DIM), per_b(rows, 1),
                  per_b(IDX_DIM, LANES), per_b(d_attn, LANES), per_b(d_attn, LANES),
                  pl.BlockSpec(memory_space=pl.ANY), pl.BlockSpec(memory_space=pl.ANY),
                  pl.BlockSpec(memory_space=pl.ANY)],
        out_specs=per_b(sn, d_attn),
        scratch_shapes=[
            pltpu.VMEM((sn, n_pages * page + ck), F32),
            pltpu.VMEM((2, IDX_DIM, ck), F32),
            pltpu.VMEM((2, d_attn, ck), F32),
            pltpu.VMEM((2, d_attn, ck), F32),
            pltpu.SemaphoreType.DMA((2,)), pltpu.SemaphoreType.DMA((2,)), pltpu.SemaphoreType.DMA((2,)),
        ],
    )
    return pl.pallas_call(
        functools.partial(_sattn_kernel, cp=cp, page=page, n_bisect=N_BISECT, n_index_bits=n_index_bits),
        grid_spec=grid_spec,
        out_shape=jax.ShapeDtypeStruct((db, sn, d_attn), F32),
        compiler_params=_cparams(1),
        name="sattn",
    )(page_table, rel_bias, q, qis, w, kin_t, kn_t, vn_t, cache_kidx_t, cache_k_t, cache_v_t)


def _gelu_tanh(x):
    return 0.5 * x * (1.0 + jnp.tanh(math.sqrt(2.0 / math.pi) * (x + 0.044715 * x * x * x)))


def _ssm_kernel(u_ref, h0r_ref, h0i_ref, lr_ref, li_ref, ldt_ref, bre_ref, bim_ref, cre_ref, cim_ref,
                d_ref, wglu_ref, o_ref, hr_out_ref, hi_out_ref,
                lbr_ref, lbi_ref, bbr_ref, bbi_ref, bur_ref, bui_ref, hre_ref, him_ref, sr_ref, si_ref,
                *, nb, unroll):
    step = pl.program_id(0)
    tt = u_ref.shape[0]
    d_ssm = u_ref.shape[1]

    @pl.when(step == 0)
    def _():
        lr, li = lr_ref[...], li_ref[...]
        dt = jnp.exp(ldt_ref[...])
        mag = jnp.exp(lr * dt)
        lbr, lbi = mag * jnp.cos(li * dt), mag * jnp.sin(li * dt)
        den = lr * lr + li * li
        xr, xi = lbr - 1.0, lbi
        cr = (xr * lr + xi * li) / den
        ci = (xi * lr - xr * li) / den
        lbr_ref[...] = lbr
        lbi_ref[...] = lbi
        bbr_ref[...] = (cr * bre_ref[...] - ci * bim_ref[...]).astype(BF16)
        bbi_ref[...] = (cr * bim_ref[...] + ci * bre_ref[...]).astype(BF16)
        sr_ref[...] = h0r_ref[...]
        si_ref[...] = h0i_ref[...]

    u = u_ref[...]
    u16 = u.astype(BF16)
    bur_ref[...] = jnp.dot(u16, bbr_ref[...], preferred_element_type=F32)
    bui_ref[...] = jnp.dot(u16, bbi_ref[...], preferred_element_type=F32)
    lbr = jnp.broadcast_to(lbr_ref[...], (nb, lbr_ref.shape[1]))
    lbi = jnp.broadcast_to(lbi_ref[...], (nb, lbi_ref.shape[1]))

    def scan(t, carry):
        hr, hi = carry
        r0 = pl.multiple_of(t * nb, nb)
        nr = lbr * hr - lbi * hi + bur_ref[pl.ds(r0, nb), :]
        ni = lbr * hi + lbi * hr + bui_ref[pl.ds(r0, nb), :]
        hre_ref[pl.ds(r0, nb), :] = nr
        him_ref[pl.ds(r0, nb), :] = ni
        return nr, ni

    hr, hi = lax.fori_loop(0, tt // nb, scan, (sr_ref[...], si_ref[...]), unroll=unroll)
    sr_ref[...] = hr
    si_ref[...] = hi
    hr_out_ref[...] = hr
    hi_out_ref[...] = hi

    y = (jnp.dot(hre_ref[...].astype(BF16), cre_ref[...], preferred_element_type=F32)
         - jnp.dot(him_ref[...].astype(BF16), cim_ref[...], preferred_element_type=F32)
         + d_ref[...] * u)
    z = _gelu_tanh(y)
    g = jnp.dot(z.astype(BF16), wglu_ref[...], preferred_element_type=F32)
    o_ref[...] = g[:, :d_ssm] / (1.0 + jnp.exp(-g[:, d_ssm:]))


def _ssm_call(u, h0r, h0i, lr, li, ldt, bre, bim, cre, cim, d, wglu, nb, tt):
    n, d_ssm = u.shape
    ns = lr.shape[1]
    assert n % tt == 0 and tt % nb == 0
    unroll = 8 if (tt // nb) % 8 == 0 else 1
    state = jax.ShapeDtypeStruct((nb, ns), F32)
    return pl.pallas_call(
        functools.partial(_ssm_kernel, nb=nb, unroll=unroll),
        grid=(n // tt,),
        in_specs=[pl.BlockSpec((tt, d_ssm), lambda i: (i, 0)),
                  _resident((nb, ns)), _resident((nb, ns)),
                  _resident((1, ns)), _resident((1, ns)), _resident((1, ns)),
                  _resident((d_ssm, ns)), _resident((d_ssm, ns)),
                  _resident((ns, d_ssm)), _resident((ns, d_ssm)),
                  _resident((1, d_ssm)), _resident((d_ssm, 2 * d_ssm))],
        out_specs=(pl.BlockSpec((tt, d_ssm), lambda i: (i, 0)),
                   pl.BlockSpec((nb, ns), lambda i: (0, 0)), pl.BlockSpec((nb, ns), lambda i: (0, 0))),
        out_shape=(jax.ShapeDtypeStruct((n, d_ssm), F32), state, state),
        scratch_shapes=[
            pltpu.VMEM((1, ns), F32), pltpu.VMEM((1, ns), F32),
            pltpu.VMEM((d_ssm, ns), BF16), pltpu.VMEM((d_ssm, ns), BF16),
            pltpu.VMEM((tt, ns), F32), pltpu.VMEM((tt, ns), F32),
            pltpu.VMEM((tt, ns), F32), pltpu.VMEM((tt, ns), F32),
            pltpu.VMEM((nb, ns), F32), pltpu.VMEM((nb, ns), F32),
        ],
        compiler_params=_cparams(1),
        name="ssm",
    )(u, h0r, h0i, lr, li, ldt, bre, bim, cre, cim, d, wglu)


def _finish_kernel(x_ref, attn_ref, ssm_ref, gi_ref, bi_ref, wo_ref, g1_ref, b1_ref, wr_ref, br_ref,
                   wg_ref, wu_ref, wd_ref, g2_ref, b2_ref, o_ref, *, alpha, pre_ln):
    x = x_ref[...]
    if pre_ln:
        x = _layer_norm(x, gi_ref[...], bi_ref[...])
    d_attn = attn_ref.shape[1]
    mix = (jnp.dot(attn_ref[...].astype(BF16), wo_ref[:d_attn, :], preferred_element_type=F32)
           + jnp.dot(ssm_ref[...].astype(BF16), wo_ref[d_attn:, :], preferred_element_type=F32))
    x1 = _layer_norm(alpha * x + mix, g1_ref[...], b1_ref[...])
    x16 = x1.astype(BF16)

    lg = jnp.dot(x16, wr_ref[...], preferred_element_type=F32) + br_ref[...]
    lane = lax.broadcasted_iota(jnp.int32, lg.shape, 1)
    big = jnp.int32(1 << 20)
    is_grp = lane < N_EXPERT_GROUPS
    gl = jnp.where(is_grp, lg, NEG_INF)
    gmax = jnp.max(gl, axis=-1, keepdims=True)
    g_sel = jnp.min(jnp.where(gl == gmax, lane, big), axis=-1, keepdims=True)
    p_grp = 1.0 / jnp.sum(jnp.where(is_grp, jnp.exp(gl - gmax), 0.0), axis=-1, keepdims=True)
    e_idx = lane - N_EXPERT_GROUPS
    in_sel = jnp.logical_and(jnp.logical_and(e_idx >= 0, e_idx < N_EXPERTS),
                             e_idx // EXPERTS_PER_GROUP == g_sel)
    el = jnp.where(in_sel, lg, NEG_INF)
    v1 = jnp.max(el, axis=-1, keepdims=True)
    i1 = jnp.min(jnp.where(el == v1, lane, big), axis=-1, keepdims=True)
    el2 = jnp.where(lane == i1, NEG_INF, el)
    v2 = jnp.max(el2, axis=-1, keepdims=True)
    i2 = jnp.min(jnp.where(el2 == v2, lane, big), axis=-1, keepdims=True)
    e21 = jnp.exp(v2 - v1)
    w1 = p_grp / (1.0 + e21)
    w2 = p_grp * e21 / (1.0 + e21)
    gates = jnp.where(lane == i1, w1, jnp.where(lane == i2, w2, 0.0))

    y = jnp.zeros(x1.shape, F32)
    for e in range(N_EXPERTS):
        gate = gates[:, N_EXPERT_GROUPS + e:N_EXPERT_GROUPS + e + 1]
        hg = jnp.dot(x16, wg_ref[e], preferred_element_type=F32)
        hu = jnp.dot(x16, wu_ref[e], preferred_element_type=F32)
        hid = hg / (1.0 + jnp.exp(-hg)) * hu * gate
        y = y + jnp.dot(hid.astype(BF16), wd_ref[e], preferred_element_type=F32)
    o_ref[...] = _layer_norm(alpha * x1 + y, g2_ref[...], b2_ref[...])


def _finish_call(x, attn, ssm, gi, bi, wo, g1, b1, wr, br, wg, wu, wd, g2, b2, alpha, pre_ln, tm):
    n, d = x.shape
    row = lambda c: pl.BlockSpec((tm, c), lambda i: (i, 0))
    return pl.pallas_call(
        functools.partial(_finish_kernel, alpha=alpha, pre_ln=pre_ln),
        grid=(n // tm,),
        in_specs=[row(d), row(attn.shape[1]), row(ssm.shape[1]),
                  _resident(gi.shape), _resident(bi.shape), _resident(wo.shape),
                  _resident(g1.shape), _resident(b1.shape), _resident(wr.shape), _resident(br.shape),
                  _resident(wg.shape), _resident(wu.shape), _resident(wd.shape),
                  _resident(g2.shape), _resident(b2.shape)],
        out_specs=row(d),
        out_shape=jax.ShapeDtypeStruct((n, d), F32),
        compiler_params=_cparams(1),
        name="finish",
    )(x, attn, ssm, gi, bi, wo, g1, b1, wr, br, wg, wu, wd, g2, b2)


def _row(a):
    return a.reshape(1, -1).astype(F32)


def _block_diag(a):
    g, r, c = a.shape
    return jnp.einsum("grc,gh->grhc", a, jnp.eye(g, dtype=a.dtype)).reshape(g * r, g * c)


def _pad_lanes(a):
    return jnp.pad(a, ((0, 0), (0, 0), (0, LANES - a.shape[2])))


def _pick_tile(n, pref):
    t = min(n, pref)
    while n % t:
        t //= 2
    return t


def kernel(x_prompt, x_sample, cache_k, cache_v, cache_kidx, state_ssm_re, state_ssm_im, page_table, ln_in_g, ln_in_b, rel_bias, w_in, idx_ln_g, idx_ln_b, ssm_lam_re, ssm_lam_im, ssm_log_dt, ssm_b_re, ssm_b_im, ssm_c_re, ssm_c_im, ssm_d, w_glu, w_out, ln_mix_g, ln_mix_b, w_router_grp, b_router_grp, w_router_exp, b_router_exp, w_gate, w_up, w_down, ln_ffn_g, ln_ffn_b):
    bsz, seq, d_model = x_prompt.shape
    dbsz, dseq, _ = x_sample.shape
    depth = w_in.shape[0]
    d_attn = N_HEADS * HEAD_DIM
    n_grp = ssm_lam_re.shape[1]
    d_ssm = n_grp * SSM_GROUP
    n_state = n_grp * SSM_STATE
    n_pool, page = cache_k.shape[1], cache_k.shape[2]
    alpha = (2.0 * depth) ** 0.25
    sizes = (d_attn, d_attn, d_attn, N_IDX_HEADS * IDX_DIM, IDX_DIM, N_IDX_HEADS, d_ssm)
    cuts = [0] + [int(c) for c in np.cumsum(sizes)]
    rel_bias = rel_bias.astype(F32)
    page_table = page_table.astype(jnp.int32)

    xp = x_prompt.reshape(bsz * seq, d_model)
    xs = x_sample.reshape(dbsz * dseq, d_model)
    outs = [[] for _ in range(10)]
    for l in range(depth):
        pre_ln = l == 0
        seg = [w_in[l][:, cuts[j]:cuts[j + 1]] for j in range(7)]
        pad = jnp.zeros((d_model, LANES - N_IDX_HEADS), w_in.dtype)
        w_row = jnp.concatenate([seg[0], seg[3], seg[6], seg[5], pad], axis=1).astype(BF16)
        w_t = jnp.concatenate([seg[1], seg[2], seg[4]], axis=1).T.astype(BF16)
        proj_w = (_row(ln_in_g), _row(ln_in_b), w_row, w_t,
                  idx_ln_g[l].reshape(IDX_DIM, 1).astype(F32), idx_ln_b[l].reshape(IDX_DIM, 1).astype(F32))
        ssm_w = (_row(ssm_lam_re[l]), _row(ssm_lam_im[l]), _row(jnp.repeat(ssm_log_dt[l], SSM_STATE)),
                 _block_diag(jnp.swapaxes(ssm_b_re[l], 1, 2)).astype(F32),
                 _block_diag(jnp.swapaxes(ssm_b_im[l], 1, 2)).astype(F32),
                 _block_diag(jnp.swapaxes(ssm_c_re[l], 1, 2)).astype(BF16),
                 _block_diag(jnp.swapaxes(ssm_c_im[l], 1, 2)).astype(BF16),
                 _row(ssm_d[l]), w_glu[l].astype(BF16))
        wr = jnp.concatenate([w_router_grp[l], jnp.transpose(w_router_exp[l], (1, 0, 2)).reshape(d_model, N_EXPERTS)], axis=1)
        wr = jnp.pad(wr, ((0, 0), (0, LANES - wr.shape[1]))).astype(BF16)
        br = jnp.pad(jnp.concatenate([b_router_grp[l], b_router_exp[l].reshape(-1)]),
                     (0, LANES - N_EXPERT_GROUPS - N_EXPERTS)).reshape(1, LANES).astype(F32)
        fin_w = (_row(ln_in_g), _row(ln_in_b), w_out[l].astype(BF16), _row(ln_mix_g[l]), _row(ln_mix_b[l]), wr, br,
                 w_gate[l].astype(BF16), w_up[l].astype(BF16), w_down[l].astype(BF16),
                 _row(ln_ffn_g[l]), _row(ln_ffn_b[l]))

        attn_l, k_l, v_l, ki_l, u_l = [], [], [], [], []
        for b in range(bsz):
            q, qi, u, ws, kt, vt, kt16, vt16, kit, kit16 = _proj_call(
                xp[b * seq:(b + 1) * seq], *proj_w, d_attn, d_ssm, pre_ln, _pick_tile(seq, PROJ_ROWS))
            attn_l.append(_pattn_call(rel_bias, q[None], qi[None], ws[None], kit16[None], kt16[None], vt16[None],
                                      tq=QUERY_BLOCK, kb=KEY_CHUNK)[0])
            k_l.append(jnp.transpose(kt.reshape(N_HEADS, HEAD_DIM, seq), (2, 0, 1)))
            v_l.append(jnp.transpose(vt.reshape(N_HEADS, HEAD_DIM, seq), (2, 0, 1)))
            ki_l.append(kit.T)
            u_l.append(u)
        u_tm = jnp.stack(u_l, axis=1).reshape(seq * bsz, d_ssm)
        h0 = jnp.zeros((bsz, n_state), F32)
        ssm_tm, hr, hi = _ssm_call(u_tm, h0, h0, *ssm_w, nb=bsz, tt=_pick_tile(seq, SSM_ROWS) * bsz)
        ssm_o = jnp.swapaxes(ssm_tm.reshape(seq, bsz, d_ssm), 0, 1).reshape(bsz * seq, d_ssm)
        xp = _finish_call(xp, jnp.concatenate(attn_l, axis=0), ssm_o, *fin_w, alpha=alpha, pre_ln=pre_ln,
                          tm=_pick_tile(bsz * seq, FINISH_ROWS))
        outs[0].append(jnp.stack(k_l))
        outs[1].append(jnp.stack(v_l))
        outs[2].append(jnp.stack(ki_l))
        outs[3].append(hr.reshape(bsz, n_grp, SSM_STATE).astype(state_ssm_re.dtype))
        outs[4].append(hi.reshape(bsz, n_grp, SSM_STATE).astype(state_ssm_im.dtype))

        n_s = dbsz * dseq
        q, qi, u, ws, kt, vt, kt16, vt16, kit, kit16 = _proj_call(xs, *proj_w, d_attn, d_ssm, pre_ln,
                                                                  _pick_tile(n_s, PROJ_ROWS))
        rows = N_IDX_HEADS * dseq
        qis = jnp.swapaxes(qi.reshape(dbsz, dseq, N_IDX_HEADS, IDX_DIM), 1, 2).reshape(dbsz, rows, IDX_DIM)
        wis = jnp.swapaxes(ws[:, :N_IDX_HEADS].reshape(dbsz, dseq, N_IDX_HEADS), 1, 2).reshape(dbsz, rows, 1)
        per_seq_t = lambda a: _pad_lanes(jnp.swapaxes(a.reshape(a.shape[0], dbsz, dseq), 0, 1))
        ck_t = jnp.transpose(cache_k[l], (0, 2, 3, 1)).reshape(n_pool, d_attn, page)
        cv_t = jnp.transpose(cache_v[l], (0, 2, 3, 1)).reshape(n_pool, d_attn, page)
        cki_t = jnp.swapaxes(cache_kidx[l], 1, 2)
        attn = _sattn_call(page_table, rel_bias, q.reshape(dbsz, dseq, d_attn), qis, wis,
                           per_seq_t(kit16[:IDX_DIM]), per_seq_t(kt16), per_seq_t(vt16), cki_t, ck_t, cv_t,
                           cp=min(PAGES_PER_CHUNK, page_table.shape[1]))
        u_tm = jnp.swapaxes(u.reshape(dbsz, dseq, d_ssm), 0, 1).reshape(n_s, d_ssm)
        ssm_tm, hr, hi = _ssm_call(u_tm, state_ssm_re[l].reshape(dbsz, n_state).astype(F32),
                                   state_ssm_im[l].reshape(dbsz, n_state).astype(F32), *ssm_w, nb=dbsz, tt=n_s)
        ssm_o = jnp.swapaxes(ssm_tm.reshape(dseq, dbsz, d_ssm), 0, 1).reshape(n_s, d_ssm)
        xs = _finish_call(xs, attn.reshape(n_s, d_attn), ssm_o, *fin_w, alpha=alpha, pre_ln=pre_ln,
                          tm=_pick_tile(n_s, FINISH_ROWS))
        outs[5].append(kt.T.reshape(dbsz, dseq, N_HEADS, HEAD_DIM))
        outs[6].append(vt.T.reshape(dbsz, dseq, N_HEADS, HEAD_DIM))
        outs[7].append(kit.T.reshape(dbsz, dseq, IDX_DIM))
        outs[8].append(hr.reshape(dbsz, n_grp, SSM_STATE).astype(state_ssm_re.dtype))
        outs[9].append(hi.reshape(dbsz, n_grp, SSM_STATE).astype(state_ssm_im.dtype))

    return (xp.reshape(bsz, seq, d_model), xs.reshape(dbsz, dseq, d_model)) + tuple(jnp.stack(o) for o in outs)
```

```python
import functools
import math

import numpy as np
import jax
import jax.numpy as jnp
from jax import lax
from jax.experimental import pallas as pl
from jax.experimental.pallas import tpu as pltpu

N_HEADS = 8
HEAD_DIM = 64
N_IDX_HEADS = 8
IDX_DIM = 64
TOPK_MAX = 256
NUM_BUCKETS = 32
MAX_DISTANCE = 128
SSM_GROUP = 16
SSM_STATE = 64
N_EXPERT_GROUPS = 4
EXPERTS_PER_GROUP = 4
N_EXPERTS = N_EXPERT_GROUPS * EXPERTS_PER_GROUP
LN_EPS = 1e-5
NEG_INF = -1e30
LOG2E = math.log2(math.e)

LANES = 128
VMEM_LIMIT = 56 * 1024 * 1024

PROJ_ROWS = 512
FINISH_ROWS = 512
SSM_ROWS = 512
QUERY_BLOCK = LANES
KEY_CHUNK = 1024
SCAN_CHUNK = 512
PAGES_PER_CHUNK = 16
N_BISECT = 16

F32 = jnp.float32
BF16 = jnp.bfloat16
_NT = (((1,), (1,)), ((), ()))


def _cparams(n_grid, vmem=VMEM_LIMIT):
    return pltpu.CompilerParams(dimension_semantics=("arbitrary",) * n_grid, vmem_limit_bytes=vmem)


def _resident(shape):
    nd = len(shape)
    return pl.BlockSpec(shape, lambda *_: (0,) * nd, pipeline_mode=pl.Buffered(1))


def _layer_norm(x, g, b):
    mu = jnp.mean(x, axis=-1, keepdims=True)
    xc = x - mu
    var = jnp.mean(xc * xc, axis=-1, keepdims=True)
    return xc * lax.rsqrt(var + LN_EPS) * g + b


def _proj_kernel(x_ref, g_ref, b_ref, wr_ref, wt_ref, ig_ref, ib_ref,
                 q_ref, qi_ref, u_ref, ws_ref, kt_ref, vt_ref, ktb_ref, vtb_ref, kit_ref, kitb_ref,
                 *, d_attn, d_ssm, pre_ln):
    xn = x_ref[...]
    if pre_ln:
        xn = _layer_norm(xn, g_ref[...], b_ref[...])
    x16 = xn.astype(BF16)
    d_qi = N_IDX_HEADS * IDX_DIM
    hr = jnp.dot(x16, wr_ref[...], preferred_element_type=F32)
    ht = lax.dot_general(wt_ref[...], x16, _NT, preferred_element_type=F32)
    q_ref[...] = (hr[:, :d_attn] * (HEAD_DIM ** -0.5 * LOG2E)).astype(BF16)
    qi_ref[...] = (hr[:, d_attn:d_attn + d_qi] * IDX_DIM ** -0.5).astype(BF16)
    u_ref[...] = hr[:, d_attn + d_qi:d_attn + d_qi + d_ssm]
    ws_ref[...] = hr[:, d_attn + d_qi + d_ssm:] * N_IDX_HEADS ** -0.5
    kt = ht[:d_attn]
    vt = ht[d_attn:2 * d_attn]
    kt_ref[...] = kt
    vt_ref[...] = vt
    ktb_ref[...] = kt.astype(BF16)
    vtb_ref[...] = vt.astype(BF16)
    ki = ht[2 * d_attn:]
    mu = jnp.mean(ki, axis=0, keepdims=True)
    kc = ki - mu
    var = jnp.mean(kc * kc, axis=0, keepdims=True)
    kin = kc * lax.rsqrt(var + LN_EPS) * ig_ref[...] + ib_ref[...]
    kit_ref[...] = kin
    kin16 = kin.astype(BF16)
    kitb_ref[...] = jnp.concatenate([kin16, kin16], axis=0)


def _proj_call(x, ln_g, ln_b, w_row, w_t, ig, ib, d_attn, d_ssm, pre_ln, tm):
    n, d = x.shape
    d_qi = N_IDX_HEADS * IDX_DIM
    row = lambda c, dt: (jax.ShapeDtypeStruct((n, c), dt), pl.BlockSpec((tm, c), lambda i: (i, 0)))
    col = lambda r, dt: (jax.ShapeDtypeStruct((r, n), dt), pl.BlockSpec((r, tm), lambda i: (0, i)))
    outs = [row(d_attn, BF16), row(d_qi, BF16), row(d_ssm, F32), row(LANES, F32),
            col(d_attn, F32), col(d_attn, F32), col(d_attn, BF16), col(d_attn, BF16),
            col(IDX_DIM, F32), col(2 * IDX_DIM, BF16)]
    return pl.pallas_call(
        functools.partial(_proj_kernel, d_attn=d_attn, d_ssm=d_ssm, pre_ln=pre_ln),
        grid=(n // tm,),
        in_specs=[pl.BlockSpec((tm, d), lambda i: (i, 0)), _resident((1, d)), _resident((1, d)),
                  _resident(w_row.shape), _resident(w_t.shape), _resident((IDX_DIM, 1)), _resident((IDX_DIM, 1))],
        out_specs=tuple(o[1] for o in outs),
        out_shape=tuple(o[0] for o in outs),
        compiler_params=_cparams(1),
        name="proj",
    )(x, ln_g, ln_b, w_row, w_t, ig, ib)


def _rel_bucket(dist):
    n = jnp.maximum(dist, 0)
    max_exact = NUM_BUCKETS // 2
    scaled = jnp.log(jnp.maximum(n, 1).astype(F32) / max_exact) / math.log(MAX_DISTANCE / max_exact)
    large = jnp.minimum(max_exact + (scaled * (NUM_BUCKETS - max_exact)).astype(jnp.int32), NUM_BUCKETS - 1)
    return jnp.where(n < max_exact, n, large)


def _shifted_bias(dist, rb_ref, h):
    bkt = _rel_bucket(dist)
    far = rb_ref[NUM_BUCKETS - 1, h]
    out = jnp.zeros(dist.shape, F32)
    for b in range(NUM_BUCKETS - 1):
        out = jnp.where(bkt == b, rb_ref[b, h] - far, out)
    return out * LOG2E


def _lane_tile_sum(x):
    acc = x[:, :LANES]
    for t in range(1, x.shape[1] // LANES):
        acc = acc + x[:, t * LANES:(t + 1) * LANES]
    return acc


def _lane_tile_max(x):
    acc = x[:, :LANES]
    for t in range(1, x.shape[1] // LANES):
        acc = jnp.maximum(acc, x[:, t * LANES:(t + 1) * LANES])
    return acc


def _lane_tile_min(x):
    acc = x[:, :LANES]
    for t in range(1, x.shape[1] // LANES):
        acc = jnp.minimum(acc, x[:, t * LANES:(t + 1) * LANES])
    return acc


def _topk_threshold(sc_ref, nch, kb, k_row, rmin, rmax, n_bisect, n_index_bits):
    rows = sc_ref.shape[0]

    def chunk(c):
        return sc_ref[:, pl.ds(pl.multiple_of(c * kb, kb), kb)]

    def count_ge(t):
        def body(c, cnt):
            return cnt + _lane_tile_sum(jnp.where(chunk(c) >= t, 1.0, 0.0))
        cnt = lax.fori_loop(0, nch, body, jnp.zeros((rows, LANES), F32))
        return jnp.sum(cnt, axis=-1, keepdims=True)

    def bisect(_, lohi):
        lo, hi = lohi
        mid = 0.5 * (lo + hi)
        ok = count_ge(mid) >= k_row
        return jnp.where(ok, mid, lo), jnp.where(ok, hi, mid)

    _, hi = lax.fori_loop(0, n_bisect, bisect, (rmin, rmax))

    def count_ge_and_next(m):
        def body(c, carry):
            cnt, nxt = carry
            x = chunk(c)
            cnt = cnt + _lane_tile_sum(jnp.where(x >= m, 1.0, 0.0))
            nxt = jnp.maximum(nxt, _lane_tile_max(jnp.where(x < m, x, NEG_INF)))
            return cnt, nxt
        cnt, nxt = lax.fori_loop(0, nch, body, (jnp.zeros((rows, LANES), F32),
                                                jnp.full((rows, LANES), NEG_INF, F32)))
        return jnp.sum(cnt, axis=-1, keepdims=True), jnp.max(nxt, axis=-1, keepdims=True)

    def first_le(t):
        def body(c, nxt):
            x = chunk(c)
            return jnp.maximum(nxt, _lane_tile_max(jnp.where(x <= t, x, NEG_INF)))
        nxt = lax.fori_loop(0, nch, body, jnp.full((rows, LANES), NEG_INF, F32))
        return jnp.max(nxt, axis=-1, keepdims=True)

    def walk_cond(st):
        return st[3] > 0.0

    def walk_body(st):
        m, cfin, done, _ = st
        cnt, nxt = count_ge_and_next(m)
        ok = jnp.logical_and(cnt >= k_row, done < 0.5)
        cfin = jnp.where(ok, cnt, cfin)
        done = jnp.where(cnt >= k_row, 1.0, done)
        m = jnp.where(done > 0.5, m, nxt)
        return m, cfin, done, jnp.sum(1.0 - done)

    zeros = jnp.zeros((rows, 1), F32)
    thr, cfin, _, _ = lax.while_loop(walk_cond, walk_body, (first_le(hi), zeros, zeros, jnp.float32(rows)))

    @pl.when(jnp.sum(jnp.where(cfin > k_row, 1.0, 0.0)) > 0.0)
    def _():
        def count_gt():
            def body(c, cnt):
                return cnt + _lane_tile_sum(jnp.where(chunk(c) > thr, 1.0, 0.0))
            cnt = lax.fori_loop(0, nch, body, jnp.zeros((rows, LANES), F32))
            return jnp.sum(cnt, axis=-1, keepdims=True)

        need = k_row - count_gt()

        def count_eq_upto(j):
            def body(c, cnt):
                x = chunk(c)
                idx = (c * kb + lax.broadcasted_iota(jnp.int32, x.shape, 1)).astype(F32)
                hit = jnp.where(x == thr, jnp.where(idx <= j, 1.0, 0.0), 0.0)
                return cnt + _lane_tile_sum(hit)
            cnt = lax.fori_loop(0, nch, body, jnp.zeros((rows, LANES), F32))
            return jnp.sum(cnt, axis=-1, keepdims=True)

        def ibisect(_, lohi):
            lo, hi_i = lohi
            mid = jnp.floor(0.5 * (lo + hi_i))
            ok = count_eq_upto(mid) >= need
            return jnp.where(ok, lo, mid), jnp.where(ok, mid, hi_i)

        width = jnp.float32(1.0) * (nch * kb)
        _, cut = lax.fori_loop(0, n_index_bits, ibisect,
                               (jnp.full((rows, 1), -1.0, F32), jnp.zeros((rows, 1), F32) + (width - 1.0)))

        def drop(c, _):
            off = pl.multiple_of(c * kb, kb)
            x = sc_ref[:, pl.ds(off, kb)]
            idx = (c * kb + lax.broadcasted_iota(jnp.int32, x.shape, 1)).astype(F32)
            dropped = jnp.where(x == thr, jnp.where(idx > cut, 1.0, 0.0), 0.0)
            sc_ref[:, pl.ds(off, kb)] = jnp.where(dropped > 0.5, NEG_INF, x)
            return 0
        lax.fori_loop(0, nch, drop, 0)

    return thr


def _pattn_kernel(rb_ref, q_ref, qi_ref, ws_ref, kit_ref, kt_ref, vt_ref, o_ref,
                  sc_ref, qm_ref, qim_ref, bias_ref, m_ref, l_ref, acc_ref, *, tq, kb, scan_kb, n_bisect, n_index_bits):
    i = pl.program_id(1)
    n_pairs = N_HEADS // 2
    nch = ((i + 1) * tq + kb - 1) // kb
    sub = kb // tq
    lane = lax.broadcasted_iota(jnp.int32, (tq, LANES), 1)
    lo_half = lane < HEAD_DIM
    q_pos = i * tq + lax.broadcasted_iota(jnp.int32, (tq, 1), 0)

    @pl.when(i == 0)
    def _():
        r = lax.broadcasted_iota(jnp.int32, (tq, tq), 0)
        c = lax.broadcasted_iota(jnp.int32, (tq, tq), 1)
        for h in range(N_HEADS):
            bias_ref[0, h] = jnp.zeros((tq, tq), F32)
            bias_ref[1, h] = _shifted_bias(r - c + tq, rb_ref, h)
            bias_ref[2, h] = _shifted_bias(r - c, rb_ref, h)

    for h in range(N_HEADS):
        keep = lo_half if h % 2 == 0 else jnp.logical_not(lo_half)
        pr = h // 2
        qm_ref[h] = jnp.where(keep, q_ref[:, pr * LANES:(pr + 1) * LANES].astype(F32), 0.0).astype(BF16)
        qim_ref[h] = jnp.where(keep, qi_ref[:, pr * LANES:(pr + 1) * LANES].astype(F32), 0.0).astype(BF16)

    wcols = [ws_ref[:, h:h + 1] for h in range(N_IDX_HEADS)]

    def score_chunk(c, carry):
        rmin, rmax = carry
        off = pl.multiple_of(c * kb, kb)
        kt = kit_ref[:, pl.ds(off, kb)]
        acc = jnp.zeros((tq, kb), F32)
        for h in range(N_IDX_HEADS):
            s = jnp.dot(qim_ref[h], kt, preferred_element_type=F32)
            acc = acc + jnp.maximum(s, 0.0) * wcols[h]
        key_pos = off + lax.broadcasted_iota(jnp.int32, (tq, kb), 1)
        valid = key_pos <= q_pos
        sc_ref[:, pl.ds(off, kb)] = jnp.where(valid, acc, NEG_INF)
        rmax = jnp.maximum(rmax, _lane_tile_max(jnp.where(valid, acc, NEG_INF)))
        rmin = jnp.minimum(rmin, _lane_tile_min(jnp.where(valid, acc, -NEG_INF)))
        return rmin, rmax

    rmin, rmax = lax.fori_loop(0, nch, score_chunk,
                               (jnp.full((tq, LANES), -NEG_INF, F32), jnp.full((tq, LANES), NEG_INF, F32)))
    rmin = jnp.min(rmin, axis=-1, keepdims=True)
    rmax = jnp.max(rmax, axis=-1, keepdims=True)

    topk = min(TOPK_MAX, kit_ref.shape[1] // 4)
    k_row = jnp.minimum(q_pos + 1, topk).astype(F32)
    nscan = ((i + 1) * tq + scan_kb - 1) // scan_kb
    thr = _topk_threshold(sc_ref, nscan, scan_kb, k_row, rmin, rmax, n_bisect, n_index_bits)

    def to_mask(c, _):
        off = pl.multiple_of(c * scan_kb, scan_kb)
        sc_ref[:, pl.ds(off, scan_kb)] = jnp.where(sc_ref[:, pl.ds(off, scan_kb)] >= thr, 0.0, NEG_INF)
        return 0
    lax.fori_loop(0, nscan, to_mask, 0)

    def logits(c, h, off, with_bias):
        pr = h // 2
        kp = kt_ref[pr * LANES:(pr + 1) * LANES, pl.ds(off, kb)]
        s = jnp.dot(qm_ref[h], kp, preferred_element_type=F32) + sc_ref[:, pl.ds(off, kb)]
        if with_bias:
            tiles = [bias_ref[jnp.clip(c * sub + t - i + 2, 0, 2), h] for t in range(sub)]
            s = s + jnp.concatenate(tiles, axis=1)
        return s

    def max_chunk(c, with_bias):
        off = pl.multiple_of(c * kb, kb)
        for h in range(N_HEADS):
            m_ref[h] = jnp.maximum(m_ref[h], _lane_tile_max(logits(c, h, off, with_bias)))

    def acc_chunk(c, with_bias):
        off = pl.multiple_of(c * kb, kb)
        for pr in range(n_pairs):
            vp = vt_ref[pr * LANES:(pr + 1) * LANES, pl.ds(off, kb)]
            for hh in range(2):
                h = 2 * pr + hh
                p = jnp.exp2(logits(c, h, off, with_bias) - jnp.concatenate([m_ref[h]] * sub, axis=1))
                l_ref[h] = l_ref[h] + _lane_tile_sum(p)
                pv = lax.dot_general(p.astype(BF16), vp, _NT, preferred_element_type=F32)
                mine = lo_half if hh == 0 else jnp.logical_not(lo_half)
                acc_ref[pr] = acc_ref[pr] + jnp.where(mine, pv, 0.0)

    def sweep(fn):
        n_far = jnp.maximum((i * tq - MAX_DISTANCE + 1) // kb, 0)

        def far_body(c, _):
            fn(c, False)
            return 0

        def near_body(c, _):
            fn(c, True)
            return 0

        lax.fori_loop(0, n_far, far_body, 0)
        lax.fori_loop(n_far, nch, near_body, 0)

    m_ref[...] = jnp.full(m_ref.shape, NEG_INF, F32)
    sweep(max_chunk)
    for h in range(N_HEADS):
        m_ref[h] = jnp.broadcast_to(jnp.max(m_ref[h], axis=-1, keepdims=True), (tq, LANES))
    l_ref[...] = jnp.zeros(l_ref.shape, F32)
    acc_ref[...] = jnp.zeros(acc_ref.shape, F32)
    sweep(acc_chunk)

    for pr in range(n_pairs):
        l0 = jnp.sum(l_ref[2 * pr], axis=-1, keepdims=True)
        l1 = jnp.sum(l_ref[2 * pr + 1], axis=-1, keepdims=True)
        o_ref[:, pr * LANES:(pr + 1) * LANES] = acc_ref[pr] / jnp.where(lo_half, l0, l1)


def _pattn_call(rel_bias, q, qi, ws, kit, kt16, vt16, tq, kb, scan_kb):
    bsz, s, d_attn = q.shape
    assert tq == LANES and MAX_DISTANCE <= tq and s % kb == 0 and kb % scan_kb == 0 and scan_kb % tq == 0
    n_index_bits = int(math.ceil(math.log2(s))) + 1
    blk = lambda c: pl.BlockSpec((None, tq, c), lambda b, i: (b, i, 0))
    whole = lambda r, c: pl.BlockSpec((None, r, c), lambda b, i: (b, 0, 0), pipeline_mode=pl.Buffered(1))
    return pl.pallas_call(
        functools.partial(_pattn_kernel, tq=tq, kb=kb, scan_kb=scan_kb, n_bisect=N_BISECT,
                          n_index_bits=n_index_bits),
        grid=(bsz, s // tq),
        in_specs=[pl.BlockSpec(memory_space=pltpu.SMEM),
                  blk(d_attn), blk(N_IDX_HEADS * IDX_DIM), blk(LANES),
                  whole(2 * IDX_DIM, s), whole(d_attn, s), whole(d_attn, s)],
        out_specs=blk(d_attn),
        out_shape=jax.ShapeDtypeStruct((bsz, s, d_attn), F32),
        scratch_shapes=[
            pltpu.VMEM((tq, s), F32),
            pltpu.VMEM((N_HEADS, tq, LANES), BF16),
            pltpu.VMEM((N_IDX_HEADS, tq, LANES), BF16),
            pltpu.VMEM((3, N_HEADS, tq, tq), F32),
            pltpu.VMEM((N_HEADS, tq, LANES), F32),
            pltpu.VMEM((N_HEADS, tq, LANES), F32),
            pltpu.VMEM((N_HEADS // 2, tq, LANES), F32),
        ],
        compiler_params=_cparams(2),
        name="pattn",
    )(rel_bias, q, qi, ws, kit, kt16, vt16)


def _sattn_kernel(pt_ref, rb_ref, q_ref, qis_ref, w_ref, kin_ref, kn_ref, vn_ref,
                  cki_hbm, ck_hbm, cv_hbm, o_ref,
                  sc_ref, ibuf, kbuf, vbuf, isem, ksem, vsem, *, cp, page, n_bisect, n_index_bits):
    b = pl.program_id(0)
    n_pages = pt_ref.shape[1]
    nc = n_pages // cp
    ck = cp * page
    past = n_pages * page
    sn = q_ref.shape[0]
    rows = N_HEADS * sn
    d_attn = q_ref.shape[1]

    def page_copies(cache, buf, sem, c, slot):
        return [pltpu.make_async_copy(cache.at[pt_ref[b, c * cp + j]],
                                      buf.at[slot, :, pl.ds(j * page, page)], sem.at[slot])
                for j in range(cp)]

    def start(copies):
        for cpy in copies:
            cpy.start()

    def wait(copies):
        for cpy in copies:
            cpy.wait()

    start(page_copies(cki_hbm, ibuf, isem, 0, 0))
    start(page_copies(ck_hbm, kbuf, ksem, 0, 0))
    start(page_copies(cv_hbm, vbuf, vsem, 0, 0))

    q_pos = past + lax.broadcasted_iota(jnp.int32, (sn, 1), 0)
    qis = qis_ref[...]
    w = w_ref[...]

    def head_sum(s):
        acc = jnp.zeros((sn, s.shape[1]), F32)
        for h in range(N_IDX_HEADS):
            acc = acc + jnp.maximum(s[h * sn:(h + 1) * sn], 0.0) * w[h * sn:(h + 1) * sn]
        return acc

    rmin = jnp.full((sn, LANES), -NEG_INF, F32)
    rmax = jnp.full((sn, LANES), NEG_INF, F32)
    for c in range(nc):
        slot = c % 2
        if c + 1 < nc:
            start(page_copies(cki_hbm, ibuf, isem, c + 1, 1 - slot))
        wait(page_copies(cki_hbm, ibuf, isem, c, slot))
        acc = head_sum(jnp.dot(qis, ibuf[slot].astype(BF16), preferred_element_type=F32))
        sc_ref[:, c * ck:(c + 1) * ck] = acc
        rmax = jnp.maximum(rmax, _lane_tile_max(acc))
        rmin = jnp.minimum(rmin, _lane_tile_min(acc))
    acc = head_sum(jnp.dot(qis, kin_ref[...], preferred_element_type=F32))
    lane = lax.broadcasted_iota(jnp.int32, (sn, LANES), 1)
    valid = jnp.logical_and(lane < sn, past + lane <= q_pos)
    sc_ref[:, past:past + LANES] = jnp.where(valid, acc, NEG_INF)
    sc_ref[:, past + LANES:past + ck] = jnp.full((sn, ck - LANES), NEG_INF, F32)
    rmax = jnp.max(jnp.maximum(rmax, jnp.where(valid, acc, NEG_INF)), axis=-1, keepdims=True)
    rmin = jnp.min(jnp.minimum(rmin, jnp.where(valid, acc, -NEG_INF)), axis=-1, keepdims=True)

    topk = min(TOPK_MAX, (past + sn) // 4)
    k_row = jnp.minimum(q_pos + 1, topk).astype(F32)
    thr = _topk_threshold(sc_ref, nc + 1, ck, k_row, rmin, rmax, n_bisect, n_index_bits)

    r_head = lax.broadcasted_iota(jnp.int32, (rows, d_attn), 0) // sn
    c_head = lax.broadcasted_iota(jnp.int32, (rows, d_attn), 1) // HEAD_DIM
    own = r_head == c_head
    qbd = jnp.where(own, jnp.tile(q_ref[...].astype(F32), (N_HEADS, 1)), 0.0).astype(BF16)

    def softmax_step(state, s, vals_t):
        m_prev, l_prev, acc_prev = state
        m_next = jnp.maximum(m_prev, jnp.max(s, axis=-1, keepdims=True))
        alpha = jnp.exp2(m_prev - m_next)
        p = jnp.exp2(s - m_next[:, :1])
        l_next = alpha * l_prev + _lane_tile_sum(p)
        pv = lax.dot_general(p.astype(BF16), vals_t, _NT, preferred_element_type=F32)
        return m_next, l_next, acc_prev * alpha[:, :1] + pv

    def biased(s, key0):
        dist = q_pos - (key0 + lax.broadcasted_iota(jnp.int32, (sn, s.shape[1]), 1))
        return s + jnp.concatenate([_shifted_bias(dist, rb_ref, h) for h in range(N_HEADS)], axis=0)

    state = (jnp.full((rows, LANES), NEG_INF, F32), jnp.zeros((rows, LANES), F32),
             jnp.zeros((rows, d_attn), F32))
    for c in range(nc):
        slot = c % 2
        if c + 1 < nc:
            start(page_copies(ck_hbm, kbuf, ksem, c + 1, 1 - slot))
            start(page_copies(cv_hbm, vbuf, vsem, c + 1, 1 - slot))
        wait(page_copies(ck_hbm, kbuf, ksem, c, slot))
        wait(page_copies(cv_hbm, vbuf, vsem, c, slot))
        madd = jnp.where(sc_ref[:, c * ck:(c + 1) * ck] >= thr, 0.0, NEG_INF)
        s = jnp.dot(qbd, kbuf[slot].astype(BF16), preferred_element_type=F32)
        s = s + jnp.tile(madd, (N_HEADS, 1))
        if (c + 1) * ck + MAX_DISTANCE > past:
            s = biased(s, c * ck)
        state = softmax_step(state, s, vbuf[slot].astype(BF16))
    madd = jnp.where(sc_ref[:, past:past + LANES] >= thr, 0.0, NEG_INF)
    s = jnp.dot(qbd, kn_ref[...], preferred_element_type=F32)
    s = biased(s + jnp.tile(madd, (N_HEADS, 1)), past)
    _, l_fin, acc = softmax_step(state, s, vn_ref[...])
    res = jnp.where(own, acc / jnp.sum(l_fin, axis=-1, keepdims=True), 0.0)
    out = res[0:sn]
    for h in range(1, N_HEADS):
        out = out + res[h * sn:(h + 1) * sn]
    o_ref[...] = out


def _sattn_call(page_table, rel_bias, q, qis, w, kin_t, kn_t, vn_t, cache_kidx_t, cache_k_t, cache_v_t, cp):
    db, sn, d_attn = q.shape
    n_pages = page_table.shape[1]
    page = cache_k_t.shape[2]
    assert n_pages % cp == 0 and sn == 8 and page == LANES and cp >= 2 and MAX_DISTANCE <= cp * page
    ck = cp * page
    rows = N_HEADS * sn
    n_index_bits = int(math.ceil(math.log2(n_pages * page + ck))) + 1
    per_b = lambda r, c: pl.BlockSpec((None, r, c), lambda b, pt: (b, 0, 0))
    grid_spec = pltpu.PrefetchScalarGridSpec(
        num_scalar_prefetch=1,
        grid=(db,),
        in_specs=[pl.BlockSpec(memory_space=pltpu.SMEM),
                  per_b(sn, d_attn), per_b(rows, IDX_DIM), per_b(rows, 1),
                  per_b(IDX_DIM, LANES), per_b(d_attn, LANES), per_b(d_attn, LANES),
                  pl.BlockSpec(memory_space=pl.ANY), pl.BlockSpec(memory_space=pl.ANY),
                  pl.BlockSpec(memory_space=pl.ANY)],
        out_specs=per_b(sn, d_attn),
        scratch_shapes=[
            pltpu.VMEM((sn, n_pages * page + ck), F32),
            pltpu.VMEM((2, IDX_DIM, ck), F32),
            pltpu.VMEM((2, d_attn, ck), F32),
            pltpu.VMEM((2, d_attn, ck), F32),
            pltpu.SemaphoreType.DMA((2,)), pltpu.SemaphoreType.DMA((2,)), pltpu.SemaphoreType.DMA((2,)),
        ],
    )
    return pl.pallas_call(
        functools.partial(_sattn_kernel, cp=cp, page=page, n_bisect=N_BISECT, n_index_bits=n_index_bits),
        grid_spec=grid_spec,
        out_shape=jax.ShapeDtypeStruct((db, sn, d_attn), F32),
        compiler_params=_cparams(1),
        name="sattn",
    )(page_table, rel_bias, q, qis, w, kin_t, kn_t, vn_t, cache_kidx_t, cache_k_t, cache_v_t)


def _gelu_tanh(x):
    return 0.5 * x * (1.0 + jnp.tanh(math.sqrt(2.0 / math.pi) * (x + 0.044715 * x * x * x)))


def _ssm_kernel(u_ref, h0r_ref, h0i_ref, lr_ref, li_ref, ldt_ref, bre_ref, bim_ref, cre_ref, cim_ref,
                d_ref, wglu_ref, o_ref, hr_out_ref, hi_out_ref,
                lbr_ref, lbi_ref, bbr_ref, bbi_ref, bur_ref, bui_ref, hre_ref, him_ref, sr_ref, si_ref,
                *, nb, unroll):
    step = pl.program_id(0)
    tt = u_ref.shape[0]
    d_ssm = u_ref.shape[1]

    @pl.when(step == 0)
    def _():
        lr, li = lr_ref[...], li_ref[...]
        dt = jnp.exp(ldt_ref[...])
        mag = jnp.exp(lr * dt)
        lbr, lbi = mag * jnp.cos(li * dt), mag * jnp.sin(li * dt)
        den = lr * lr + li * li
        xr, xi = lbr - 1.0, lbi
        cr = (xr * lr + xi * li) / den
        ci = (xi * lr - xr * li) / den
        lbr_ref[...] = lbr
        lbi_ref[...] = lbi
        bbr_ref[...] = (cr * bre_ref[...] - ci * bim_ref[...]).astype(BF16)
        bbi_ref[...] = (cr * bim_ref[...] + ci * bre_ref[...]).astype(BF16)
        sr_ref[...] = h0r_ref[...]
        si_ref[...] = h0i_ref[...]

    u = u_ref[...]
    u16 = u.astype(BF16)
    bur_ref[...] = jnp.dot(u16, bbr_ref[...], preferred_element_type=F32)
    bui_ref[...] = jnp.dot(u16, bbi_ref[...], preferred_element_type=F32)
    lbr = jnp.broadcast_to(lbr_ref[...], (nb, lbr_ref.shape[1]))
    lbi = jnp.broadcast_to(lbi_ref[...], (nb, lbi_ref.shape[1]))

    def scan(t, carry):
        hr, hi = carry
        r0 = pl.multiple_of(t * nb, nb)
        nr = lbr * hr - lbi * hi + bur_ref[pl.ds(r0, nb), :]
        ni = lbr * hi + lbi * hr + bui_ref[pl.ds(r0, nb), :]
        hre_ref[pl.ds(r0, nb), :] = nr
        him_ref[pl.ds(r0, nb), :] = ni
        return nr, ni

    hr, hi = lax.fori_loop(0, tt // nb, scan, (sr_ref[...], si_ref[...]), unroll=unroll)
    sr_ref[...] = hr
    si_ref[...] = hi
    hr_out_ref[...] = hr
    hi_out_ref[...] = hi

    y = (jnp.dot(hre_ref[...].astype(BF16), cre_ref[...], preferred_element_type=F32)
         - jnp.dot(him_ref[...].astype(BF16), cim_ref[...], preferred_element_type=F32)
         + d_ref[...] * u)
    z = _gelu_tanh(y)
    g = jnp.dot(z.astype(BF16), wglu_ref[...], preferred_element_type=F32)
    o_ref[...] = g[:, :d_ssm] / (1.0 + jnp.exp(-g[:, d_ssm:]))


def _ssm_call(u, h0r, h0i, lr, li, ldt, bre, bim, cre, cim, d, wglu, nb, tt):
    n, d_ssm = u.shape
    ns = lr.shape[1]
    assert n % tt == 0 and tt % nb == 0
    unroll = 8 if (tt // nb) % 8 == 0 else 1
    state = jax.ShapeDtypeStruct((nb, ns), F32)
    return pl.pallas_call(
        functools.partial(_ssm_kernel, nb=nb, unroll=unroll),
        grid=(n // tt,),
        in_specs=[pl.BlockSpec((tt, d_ssm), lambda i: (i, 0)),
                  _resident((nb, ns)), _resident((nb, ns)),
                  _resident((1, ns)), _resident((1, ns)), _resident((1, ns)),
                  _resident((d_ssm, ns)), _resident((d_ssm, ns)),
                  _resident((ns, d_ssm)), _resident((ns, d_ssm)),
                  _resident((1, d_ssm)), _resident((d_ssm, 2 * d_ssm))],
        out_specs=(pl.BlockSpec((tt, d_ssm), lambda i: (i, 0)),
                   pl.BlockSpec((nb, ns), lambda i: (0, 0)), pl.BlockSpec((nb, ns), lambda i: (0, 0))),
        out_shape=(jax.ShapeDtypeStruct((n, d_ssm), F32), state, state),
        scratch_shapes=[
            pltpu.VMEM((1, ns), F32), pltpu.VMEM((1, ns), F32),
            pltpu.VMEM((d_ssm, ns), BF16), pltpu.VMEM((d_ssm, ns), BF16),
            pltpu.VMEM((tt, ns), F32), pltpu.VMEM((tt, ns), F32),
            pltpu.VMEM((tt, ns), F32), pltpu.VMEM((tt, ns), F32),
            pltpu.VMEM((nb, ns), F32), pltpu.VMEM((nb, ns), F32),
        ],
        compiler_params=_cparams(1),
        name="ssm",
    )(u, h0r, h0i, lr, li, ldt, bre, bim, cre, cim, d, wglu)


def _finish_kernel(x_ref, attn_ref, ssm_ref, gi_ref, bi_ref, wo_ref, g1_ref, b1_ref, wr_ref, br_ref,
                   wg_ref, wu_ref, wd_ref, g2_ref, b2_ref, o_ref, *, alpha, pre_ln):
    x = x_ref[...]
    if pre_ln:
        x = _layer_norm(x, gi_ref[...], bi_ref[...])
    d_attn = attn_ref.shape[1]
    mix = (jnp.dot(attn_ref[...].astype(BF16), wo_ref[:d_attn, :], preferred_element_type=F32)
           + jnp.dot(ssm_ref[...].astype(BF16), wo_ref[d_attn:, :], preferred_element_type=F32))
    x1 = _layer_norm(alpha * x + mix, g1_ref[...], b1_ref[...])
    x16 = x1.astype(BF16)

    lg = jnp.dot(x16, wr_ref[...], preferred_element_type=F32) + br_ref[...]
    lane = lax.broadcasted_iota(jnp.int32, lg.shape, 1)
    big = jnp.int32(1 << 20)
    is_grp = lane < N_EXPERT_GROUPS
    gl = jnp.where(is_grp, lg, NEG_INF)
    gmax = jnp.max(gl, axis=-1, keepdims=True)
    g_sel = jnp.min(jnp.where(gl == gmax, lane, big), axis=-1, keepdims=True)
    p_grp = 1.0 / jnp.sum(jnp.where(is_grp, jnp.exp(gl - gmax), 0.0), axis=-1, keepdims=True)
    e_idx = lane - N_EXPERT_GROUPS
    in_sel = jnp.logical_and(jnp.logical_and(e_idx >= 0, e_idx < N_EXPERTS),
                             e_idx // EXPERTS_PER_GROUP == g_sel)
    el = jnp.where(in_sel, lg, NEG_INF)
    v1 = jnp.max(el, axis=-1, keepdims=True)
    i1 = jnp.min(jnp.where(el == v1, lane, big), axis=-1, keepdims=True)
    el2 = jnp.where(lane == i1, NEG_INF, el)
    v2 = jnp.max(el2, axis=-1, keepdims=True)
    i2 = jnp.min(jnp.where(el2 == v2, lane, big), axis=-1, keepdims=True)
    e21 = jnp.exp(v2 - v1)
    w1 = p_grp / (1.0 + e21)
    w2 = p_grp * e21 / (1.0 + e21)
    gates = jnp.where(lane == i1, w1, jnp.where(lane == i2, w2, 0.0))

    y = jnp.zeros(x1.shape, F32)
    for e in range(N_EXPERTS):
        gate = gates[:, N_EXPERT_GROUPS + e:N_EXPERT_GROUPS + e + 1]
        hg = jnp.dot(x16, wg_ref[e], preferred_element_type=F32)
        hu = jnp.dot(x16, wu_ref[e], preferred_element_type=F32)
        hid = hg / (1.0 + jnp.exp(-hg)) * hu * gate
        y = y + jnp.dot(hid.astype(BF16), wd_ref[e], preferred_element_type=F32)
    o_ref[...] = _layer_norm(alpha * x1 + y, g2_ref[...], b2_ref[...])


def _finish_call(x, attn, ssm, gi, bi, wo, g1, b1, wr, br, wg, wu, wd, g2, b2, alpha, pre_ln, tm):
    n, d = x.shape
    row = lambda c: pl.BlockSpec((tm, c), lambda i: (i, 0))
    return pl.pallas_call(
        functools.partial(_finish_kernel, alpha=alpha, pre_ln=pre_ln),
        grid=(n // tm,),
        in_specs=[row(d), row(attn.shape[1]), row(ssm.shape[1]),
                  _resident(gi.shape), _resident(bi.shape), _resident(wo.shape),
                  _resident(g1.shape), _resident(b1.shape), _resident(wr.shape), _resident(br.shape),
                  _resident(wg.shape), _resident(wu.shape), _resident(wd.shape),
                  _resident(g2.shape), _resident(b2.shape)],
        out_specs=row(d),
        out_shape=jax.ShapeDtypeStruct((n, d), F32),
        compiler_params=_cparams(1),
        name="finish",
    )(x, attn, ssm, gi, bi, wo, g1, b1, wr, br, wg, wu, wd, g2, b2)


def _row(a):
    return a.reshape(1, -1).astype(F32)


def _block_diag(a):
    g, r, c = a.shape
    return jnp.einsum("grc,gh->grhc", a, jnp.eye(g, dtype=a.dtype)).reshape(g * r, g * c)


def _pad_lanes(a):
    return jnp.pad(a, ((0, 0), (0, 0), (0, LANES - a.shape[2])))


def _pick_tile(n, pref):
    t = min(n, pref)
    while n % t:
        t //= 2
    return t


def kernel(x_prompt, x_sample, cache_k, cache_v, cache_kidx, state_ssm_re, state_ssm_im, page_table, ln_in_g, ln_in_b, rel_bias, w_in, idx_ln_g, idx_ln_b, ssm_lam_re, ssm_lam_im, ssm_log_dt, ssm_b_re, ssm_b_im, ssm_c_re, ssm_c_im, ssm_d, w_glu, w_out, ln_mix_g, ln_mix_b, w_router_grp, b_router_grp, w_router_exp, b_router_exp, w_gate, w_up, w_down, ln_ffn_g, ln_ffn_b):
    bsz, seq, d_model = x_prompt.shape
    dbsz, dseq, _ = x_sample.shape
    depth = w_in.shape[0]
    d_attn = N_HEADS * HEAD_DIM
    n_grp = ssm_lam_re.shape[1]
    d_ssm = n_grp * SSM_GROUP
    n_state = n_grp * SSM_STATE
    n_pool, page = cache_k.shape[1], cache_k.shape[2]
    alpha = (2.0 * depth) ** 0.25
    sizes = (d_attn, d_attn, d_attn, N_IDX_HEADS * IDX_DIM, IDX_DIM, N_IDX_HEADS, d_ssm)
    cuts = [0] + [int(c) for c in np.cumsum(sizes)]
    rel_bias = rel_bias.astype(F32)
    page_table = page_table.astype(jnp.int32)

    xp = x_prompt.reshape(bsz * seq, d_model)
    xs = x_sample.reshape(dbsz * dseq, d_model)
    outs = [[] for _ in range(10)]
    for l in range(depth):
        pre_ln = l == 0
        seg = [w_in[l][:, cuts[j]:cuts[j + 1]] for j in range(7)]
        pad = jnp.zeros((d_model, LANES - N_IDX_HEADS), w_in.dtype)
        w_row = jnp.concatenate([seg[0], seg[3], seg[6], seg[5], pad], axis=1).astype(BF16)
        w_t = jnp.concatenate([seg[1], seg[2], seg[4]], axis=1).T.astype(BF16)
        proj_w = (_row(ln_in_g), _row(ln_in_b), w_row, w_t,
                  idx_ln_g[l].reshape(IDX_DIM, 1).astype(F32), idx_ln_b[l].reshape(IDX_DIM, 1).astype(F32))
        ssm_w = (_row(ssm_lam_re[l]), _row(ssm_lam_im[l]), _row(jnp.repeat(ssm_log_dt[l], SSM_STATE)),
                 _block_diag(jnp.swapaxes(ssm_b_re[l], 1, 2)).astype(F32),
                 _block_diag(jnp.swapaxes(ssm_b_im[l], 1, 2)).astype(F32),
                 _block_diag(jnp.swapaxes(ssm_c_re[l], 1, 2)).astype(BF16),
                 _block_diag(jnp.swapaxes(ssm_c_im[l], 1, 2)).astype(BF16),
                 _row(ssm_d[l]), w_glu[l].astype(BF16))
        wr = jnp.concatenate([w_router_grp[l], jnp.transpose(w_router_exp[l], (1, 0, 2)).reshape(d_model, N_EXPERTS)], axis=1)
        wr = jnp.pad(wr, ((0, 0), (0, LANES - wr.shape[1]))).astype(BF16)
        br = jnp.pad(jnp.concatenate([b_router_grp[l], b_router_exp[l].reshape(-1)]),
                     (0, LANES - N_EXPERT_GROUPS - N_EXPERTS)).reshape(1, LANES).astype(F32)
        fin_w = (_row(ln_in_g), _row(ln_in_b), w_out[l].astype(BF16), _row(ln_mix_g[l]), _row(ln_mix_b[l]), wr, br,
                 w_gate[l].astype(BF16), w_up[l].astype(BF16), w_down[l].astype(BF16),
                 _row(ln_ffn_g[l]), _row(ln_ffn_b[l]))

        attn_l, k_l, v_l, ki_l, u_l = [], [], [], [], []
        for b in range(bsz):
            q, qi, u, ws, kt, vt, kt16, vt16, kit, kit16 = _proj_call(
                xp[b * seq:(b + 1) * seq], *proj_w, d_attn, d_ssm, pre_ln, _pick_tile(seq, PROJ_ROWS))
            attn_l.append(_pattn_call(rel_bias, q[None], qi[None], ws[None], kit16[None], kt16[None], vt16[None],
                                      tq=QUERY_BLOCK, kb=KEY_CHUNK, scan_kb=SCAN_CHUNK)[0])
            k_l.append(jnp.transpose(kt.reshape(N_HEADS, HEAD_DIM, seq), (2, 0, 1)))
            v_l.append(jnp.transpose(vt.reshape(N_HEADS, HEAD_DIM, seq), (2, 0, 1)))
            ki_l.append(kit.T)
            u_l.append(u)
        u_tm = jnp.stack(u_l, axis=1).reshape(seq * bsz, d_ssm)
        h0 = jnp.zeros((bsz, n_state), F32)
        ssm_tm, hr, hi = _ssm_call(u_tm, h0, h0, *ssm_w, nb=bsz, tt=_pick_tile(seq, SSM_ROWS) * bsz)
        ssm_o = jnp.swapaxes(ssm_tm.reshape(seq, bsz, d_ssm), 0, 1).reshape(bsz * seq, d_ssm)
        xp = _finish_call(xp, jnp.concatenate(attn_l, axis=0), ssm_o, *fin_w, alpha=alpha, pre_ln=pre_ln,
                          tm=_pick_tile(bsz * seq, FINISH_ROWS))
        outs[0].append(jnp.stack(k_l))
        outs[1].append(jnp.stack(v_l))
        outs[2].append(jnp.stack(ki_l))
        outs[3].append(hr.reshape(bsz, n_grp, SSM_STATE).astype(state_ssm_re.dtype))
        outs[4].append(hi.reshape(bsz, n_grp, SSM_STATE).astype(state_ssm_im.dtype))

        n_s = dbsz * dseq
        q, qi, u, ws, kt, vt, kt16, vt16, kit, kit16 = _proj_call(xs, *proj_w, d_attn, d_ssm, pre_ln,
                                                                  _pick_tile(n_s, PROJ_ROWS))
        rows = N_IDX_HEADS * dseq
        qis = jnp.swapaxes(qi.reshape(dbsz, dseq, N_IDX_HEADS, IDX_DIM), 1, 2).reshape(dbsz, rows, IDX_DIM)
        wis = jnp.swapaxes(ws[:, :N_IDX_HEADS].reshape(dbsz, dseq, N_IDX_HEADS), 1, 2).reshape(dbsz, rows, 1)
        per_seq_t = lambda a: _pad_lanes(jnp.swapaxes(a.reshape(a.shape[0], dbsz, dseq), 0, 1))
        ck_t = jnp.transpose(cache_k[l], (0, 2, 3, 1)).reshape(n_pool, d_attn, page)
        cv_t = jnp.transpose(cache_v[l], (0, 2, 3, 1)).reshape(n_pool, d_attn, page)
        cki_t = jnp.swapaxes(cache_kidx[l], 1, 2)
        attn = _sattn_call(page_table, rel_bias, q.reshape(dbsz, dseq, d_attn), qis, wis,
                           per_seq_t(kit16[:IDX_DIM]), per_seq_t(kt16), per_seq_t(vt16), cki_t, ck_t, cv_t,
                           cp=min(PAGES_PER_CHUNK, page_table.shape[1]))
        u_tm = jnp.swapaxes(u.reshape(dbsz, dseq, d_ssm), 0, 1).reshape(n_s, d_ssm)
        ssm_tm, hr, hi = _ssm_call(u_tm, state_ssm_re[l].reshape(dbsz, n_state).astype(F32),
                                   state_ssm_im[l].reshape(dbsz, n_state).astype(F32), *ssm_w, nb=dbsz, tt=n_s)
        ssm_o = jnp.swapaxes(ssm_tm.reshape(dseq, dbsz, d_ssm), 0, 1).reshape(n_s, d_ssm)
        xs = _finish_call(xs, attn.reshape(n_s, d_attn), ssm_o, *fin_w, alpha=alpha, pre_ln=pre_ln,
                          tm=_pick_tile(n_s, FINISH_ROWS))
        outs[5].append(kt.T.reshape(dbsz, dseq, N_HEADS, HEAD_DIM))
        outs[6].append(vt.T.reshape(dbsz, dseq, N_HEADS, HEAD_DIM))
        outs[7].append(kit.T.reshape(dbsz, dseq, IDX_DIM))
        outs[8].append(hr.reshape(dbsz, n_grp, SSM_STATE).astype(state_ssm_re.dtype))
        outs[9].append(hi.reshape(dbsz, n_grp, SSM_STATE).astype(state_ssm_im.dtype))

    return (xp.reshape(bsz, seq, d_model), xs.reshape(dbsz, dseq, d_model)) + tuple(jnp.stack(o) for o in outs)
```

```python
import functools
import math

import numpy as np
import jax
import jax.numpy as jnp
from jax import lax
from jax.experimental import pallas as pl
from jax.experimental.pallas import tpu as pltpu

N_HEADS = 8
HEAD_DIM = 64
N_IDX_HEADS = 8
IDX_DIM = 64
TOPK_MAX = 256
NUM_BUCKETS = 32
MAX_DISTANCE = 128
SSM_GROUP = 16
SSM_STATE = 64
N_EXPERT_GROUPS = 4
EXPERTS_PER_GROUP = 4
N_EXPERTS = N_EXPERT_GROUPS * EXPERTS_PER_GROUP
LN_EPS = 1e-5
NEG_INF = -1e30
LOG2E = math.log2(math.e)

LANES = 128
VMEM_LIMIT = 56 * 1024 * 1024

PROJ_ROWS = 512
FINISH_ROWS = 512
SSM_ROWS = 512
QUERY_BLOCK = LANES
KEY_CHUNK = 1024
SCAN_CHUNK = 512
PAGES_PER_CHUNK = 16
KV_SLOTS = 4
N_BISECT = 16
N_COARSE = 10

F32 = jnp.float32
BF16 = jnp.bfloat16
_NT = (((1,), (1,)), ((), ()))


def _cparams(n_grid, vmem=VMEM_LIMIT):
    return pltpu.CompilerParams(dimension_semantics=("arbitrary",) * n_grid, vmem_limit_bytes=vmem)


def _resident(shape):
    nd = len(shape)
    return pl.BlockSpec(shape, lambda *_: (0,) * nd, pipeline_mode=pl.Buffered(1))


def _layer_norm(x, g, b):
    mu = jnp.mean(x, axis=-1, keepdims=True)
    xc = x - mu
    var = jnp.mean(xc * xc, axis=-1, keepdims=True)
    return xc * lax.rsqrt(var + LN_EPS) * g + b


def _proj_kernel(x_ref, g_ref, b_ref, wr_ref, wt_ref, ig_ref, ib_ref,
                 q_ref, qi_ref, u_ref, ws_ref, kt_ref, vt_ref, ktb_ref, vtb_ref, kit_ref, kitb_ref,
                 *, d_attn, d_ssm, pre_ln):
    xn = x_ref[...]
    if pre_ln:
        xn = _layer_norm(xn, g_ref[...], b_ref[...])
    x16 = xn.astype(BF16)
    d_qi = N_IDX_HEADS * IDX_DIM
    hr = jnp.dot(x16, wr_ref[...], preferred_element_type=F32)
    ht = lax.dot_general(wt_ref[...], x16, _NT, preferred_element_type=F32)
    q_ref[...] = (hr[:, :d_attn] * (HEAD_DIM ** -0.5 * LOG2E)).astype(BF16)
    qi_ref[...] = (hr[:, d_attn:d_attn + d_qi] * IDX_DIM ** -0.5).astype(BF16)
    u_ref[...] = hr[:, d_attn + d_qi:d_attn + d_qi + d_ssm]
    ws_ref[...] = hr[:, d_attn + d_qi + d_ssm:] * N_IDX_HEADS ** -0.5
    kt = ht[:d_attn]
    vt = ht[d_attn:2 * d_attn]
    kt_ref[...] = kt
    vt_ref[...] = vt
    ktb_ref[...] = kt.astype(BF16)
    vtb_ref[...] = vt.astype(BF16)
    ki = ht[2 * d_attn:]
    mu = jnp.mean(ki, axis=0, keepdims=True)
    kc = ki - mu
    var = jnp.mean(kc * kc, axis=0, keepdims=True)
    kin = kc * lax.rsqrt(var + LN_EPS) * ig_ref[...] + ib_ref[...]
    kit_ref[...] = kin
    kin16 = kin.astype(BF16)
    kitb_ref[...] = jnp.concatenate([kin16, kin16], axis=0)


def _proj_call(x, ln_g, ln_b, w_row, w_t, ig, ib, d_attn, d_ssm, pre_ln, tm):
    n, d = x.shape
    d_qi = N_IDX_HEADS * IDX_DIM
    row = lambda c, dt: (jax.ShapeDtypeStruct((n, c), dt), pl.BlockSpec((tm, c), lambda i: (i, 0)))
    col = lambda r, dt: (jax.ShapeDtypeStruct((r, n), dt), pl.BlockSpec((r, tm), lambda i: (0, i)))
    outs = [row(d_attn, BF16), row(d_qi, BF16), row(d_ssm, F32), row(LANES, F32),
            col(d_attn, F32), col(d_attn, F32), col(d_attn, BF16), col(d_attn, BF16),
            col(IDX_DIM, F32), col(2 * IDX_DIM, BF16)]
    return pl.pallas_call(
        functools.partial(_proj_kernel, d_attn=d_attn, d_ssm=d_ssm, pre_ln=pre_ln),
        grid=(n // tm,),
        in_specs=[pl.BlockSpec((tm, d), lambda i: (i, 0)), _resident((1, d)), _resident((1, d)),
                  _resident(w_row.shape), _resident(w_t.shape), _resident((IDX_DIM, 1)), _resident((IDX_DIM, 1))],
        out_specs=tuple(o[1] for o in outs),
        out_shape=tuple(o[0] for o in outs),
        compiler_params=_cparams(1),
        name="proj",
    )(x, ln_g, ln_b, w_row, w_t, ig, ib)


def _rel_bucket(dist):
    n = jnp.maximum(dist, 0)
    max_exact = NUM_BUCKETS // 2
    scaled = jnp.log(jnp.maximum(n, 1).astype(F32) / max_exact) / math.log(MAX_DISTANCE / max_exact)
    large = jnp.minimum(max_exact + jnp.floor(scaled * (NUM_BUCKETS - max_exact)).astype(jnp.int32),
                        NUM_BUCKETS - 1)
    return jnp.where(n < max_exact, n, large)


def _shifted_bias(dist, rb_ref, h):
    bkt = _rel_bucket(dist)
    far = rb_ref[NUM_BUCKETS - 1, h]
    out = jnp.zeros(dist.shape, F32)
    for b in range(NUM_BUCKETS - 1):
        out = jnp.where(bkt == b, rb_ref[b, h] - far, out)
    return out * LOG2E


def _lane_tile_sum(x):
    acc = x[:, :LANES]
    for t in range(1, x.shape[1] // LANES):
        acc = acc + x[:, t * LANES:(t + 1) * LANES]
    return acc


def _lane_tile_max(x):
    acc = x[:, :LANES]
    for t in range(1, x.shape[1] // LANES):
        acc = jnp.maximum(acc, x[:, t * LANES:(t + 1) * LANES])
    return acc


def _lane_tile_min(x):
    acc = x[:, :LANES]
    for t in range(1, x.shape[1] // LANES):
        acc = jnp.minimum(acc, x[:, t * LANES:(t + 1) * LANES])
    return acc


def _chunk_of(ref, kb):
    return lambda c: ref[:, pl.ds(pl.multiple_of(c * kb, kb), kb)]


def _count_ge(ref, nch, kb, t):
    chunk = _chunk_of(ref, kb)

    def body(c, cnt):
        return cnt + _lane_tile_sum(jnp.where(chunk(c) >= t, 1.0, 0.0))
    cnt = lax.fori_loop(0, nch, body, jnp.zeros((ref.shape[0], LANES), F32))
    return jnp.sum(cnt, axis=-1, keepdims=True)


def _kth_largest(ref, nch, kb, k_row, rmin, rmax, n_bisect):
    rows = ref.shape[0]
    chunk = _chunk_of(ref, kb)

    def bisect(_, lohi):
        lo, hi = lohi
        mid = 0.5 * (lo + hi)
        ok = _count_ge(ref, nch, kb, mid) >= k_row
        return jnp.where(ok, mid, lo), jnp.where(ok, hi, mid)

    _, hi = lax.fori_loop(0, n_bisect, bisect, (rmin, rmax))

    def count_ge_and_next(m):
        def body(c, carry):
            cnt, nxt = carry
            x = chunk(c)
            cnt = cnt + _lane_tile_sum(jnp.where(x >= m, 1.0, 0.0))
            nxt = jnp.maximum(nxt, _lane_tile_max(jnp.where(x < m, x, NEG_INF)))
            return cnt, nxt
        cnt, nxt = lax.fori_loop(0, nch, body, (jnp.zeros((rows, LANES), F32),
                                                jnp.full((rows, LANES), NEG_INF, F32)))
        return jnp.sum(cnt, axis=-1, keepdims=True), jnp.max(nxt, axis=-1, keepdims=True)

    def first_le(t):
        def body(c, nxt):
            x = chunk(c)
            return jnp.maximum(nxt, _lane_tile_max(jnp.where(x <= t, x, NEG_INF)))
        nxt = lax.fori_loop(0, nch, body, jnp.full((rows, LANES), NEG_INF, F32))
        return jnp.max(nxt, axis=-1, keepdims=True)

    def walk_cond(st):
        return st[3] > 0.0

    def walk_body(st):
        m, cfin, done, _ = st
        cnt, nxt = count_ge_and_next(m)
        ok = jnp.logical_and(cnt >= k_row, done < 0.5)
        cfin = jnp.where(ok, cnt, cfin)
        done = jnp.where(cnt >= k_row, 1.0, done)
        m = jnp.where(done > 0.5, m, nxt)
        return m, cfin, done, jnp.sum(1.0 - done)

    zeros = jnp.zeros((rows, 1), F32)
    thr, cfin, _, _ = lax.while_loop(walk_cond, walk_body, (first_le(hi), zeros, zeros, jnp.float32(rows)))
    return thr, cfin


def _drop_ties(ref, nch, kb, thr, k_row, n_index_bits):
    rows = ref.shape[0]
    chunk = _chunk_of(ref, kb)

    def count_gt():
        def body(c, cnt):
            return cnt + _lane_tile_sum(jnp.where(chunk(c) > thr, 1.0, 0.0))
        cnt = lax.fori_loop(0, nch, body, jnp.zeros((rows, LANES), F32))
        return jnp.sum(cnt, axis=-1, keepdims=True)

    need = k_row - count_gt()

    def count_eq_upto(j):
        def body(c, cnt):
            x = chunk(c)
            idx = (c * kb + lax.broadcasted_iota(jnp.int32, x.shape, 1)).astype(F32)
            hit = jnp.where(x == thr, jnp.where(idx <= j, 1.0, 0.0), 0.0)
            return cnt + _lane_tile_sum(hit)
        cnt = lax.fori_loop(0, nch, body, jnp.zeros((rows, LANES), F32))
        return jnp.sum(cnt, axis=-1, keepdims=True)

    def ibisect(_, lohi):
        lo, hi_i = lohi
        mid = jnp.floor(0.5 * (lo + hi_i))
        ok = count_eq_upto(mid) >= need
        return jnp.where(ok, lo, mid), jnp.where(ok, mid, hi_i)

    width = jnp.float32(1.0) * (nch * kb)
    _, cut = lax.fori_loop(0, n_index_bits, ibisect,
                           (jnp.full((rows, 1), -1.0, F32), jnp.zeros((rows, 1), F32) + (width - 1.0)))

    def drop(c, _):
        off = pl.multiple_of(c * kb, kb)
        x = ref[:, pl.ds(off, kb)]
        idx = (c * kb + lax.broadcasted_iota(jnp.int32, x.shape, 1)).astype(F32)
        dropped = jnp.where(x == thr, jnp.where(idx > cut, 1.0, 0.0), 0.0)
        ref[:, pl.ds(off, kb)] = jnp.where(dropped > 0.5, NEG_INF, x)
        return 0
    lax.fori_loop(0, nch, drop, 0)


def _topk_threshold(sc_ref, nch, kb, k_row, rmin, rmax, n_bisect, n_index_bits):
    thr, cfin = _kth_largest(sc_ref, nch, kb, k_row, rmin, rmax, n_bisect)

    @pl.when(jnp.sum(jnp.where(cfin > k_row, 1.0, 0.0)) > 0.0)
    def _():
        _drop_ties(sc_ref, nch, kb, thr, k_row, n_index_bits)

    return thr


def _topk_threshold_two_level(sc_ref, cand_ref, res_ref, nch, kb, k_row, n_valid, rmin, rmax,
                              n_coarse, n_bisect, n_index_bits):
    rows = sc_ref.shape[0]
    chunk = _chunk_of(sc_ref, kb)
    hi0 = rmax + jnp.maximum(jnp.abs(rmax), 1e-30) * 2.0 ** -20

    def bisect(_, st):
        lo, hi, c_lo, c_hi = st
        mid = 0.5 * (lo + hi)
        c = _count_ge(sc_ref, nch, kb, mid)
        ok = c >= k_row
        return (jnp.where(ok, mid, lo), jnp.where(ok, hi, mid), jnp.where(ok, c, c_lo), jnp.where(ok, c_hi, c))

    lo, hi, c_lo, c_hi = lax.fori_loop(0, n_coarse, bisect, (rmin, hi0, n_valid, jnp.zeros((rows, 1), F32)))

    def capture(c, top):
        a, b, d = top
        x = chunk(c)
        x = jnp.where(x < hi, x, NEG_INF)
        for t in range(kb // LANES):
            v = x[:, t * LANES:(t + 1) * LANES]
            a, v = jnp.maximum(a, v), jnp.minimum(a, v)
            b, v = jnp.maximum(b, v), jnp.minimum(b, v)
            d = jnp.maximum(d, v)
        return a, b, d

    neg = jnp.full((rows, LANES), NEG_INF, F32)
    a, b, d = lax.fori_loop(0, nch, capture, (neg, neg, neg))
    cand_ref[:, 0:LANES] = a
    cand_ref[:, LANES:2 * LANES] = b
    cand_ref[:, 2 * LANES:3 * LANES] = d
    kept = _count_ge(cand_ref, 1, 3 * LANES, lo)
    missed = jnp.sum(jnp.where(kept == c_lo - c_hi, 0.0, 1.0))

    @pl.when(missed == 0.0)
    def _():
        thr, cnt = _kth_largest(cand_ref, 1, 3 * LANES, k_row - c_hi, lo, hi, n_bisect)
        res_ref[:, 0:LANES] = jnp.broadcast_to(thr, (rows, LANES))
        res_ref[:, LANES:2 * LANES] = jnp.broadcast_to(cnt + c_hi, (rows, LANES))

    @pl.when(missed > 0.0)
    def _():
        thr, cnt = _kth_largest(sc_ref, nch, kb, k_row, rmin, rmax, n_bisect)
        res_ref[:, 0:LANES] = jnp.broadcast_to(thr, (rows, LANES))
        res_ref[:, LANES:2 * LANES] = jnp.broadcast_to(cnt, (rows, LANES))

    thr = res_ref[:, 0:1]
    cfin = res_ref[:, LANES:LANES + 1]

    @pl.when(jnp.sum(jnp.where(cfin > k_row, 1.0, 0.0)) > 0.0)
    def _():
        _drop_ties(sc_ref, nch, kb, thr, k_row, n_index_bits)

    return thr


def _pattn_kernel(rb_ref, q_ref, qi_ref, ws_ref, kit_ref, kt_ref, vt_ref, o_ref,
                  sc_ref, cand_ref, res_ref, qm_ref, qim_ref, bias_ref, m_ref, l_ref, acc_ref,
                  *, tq, kb, scan_kb, n_bisect, n_index_bits):
    i = pl.program_id(1)
    n_pairs = N_HEADS // 2
    nch = ((i + 1) * tq + kb - 1) // kb
    sub = kb // tq
    lane = lax.broadcasted_iota(jnp.int32, (tq, LANES), 1)
    lo_half = lane < HEAD_DIM
    q_pos = i * tq + lax.broadcasted_iota(jnp.int32, (tq, 1), 0)

    @pl.when(i == 0)
    def _():
        r = lax.broadcasted_iota(jnp.int32, (tq, tq), 0)
        c = lax.broadcasted_iota(jnp.int32, (tq, tq), 1)
        for h in range(N_HEADS):
            bias_ref[0, h] = jnp.zeros((tq, tq), F32)
            bias_ref[1, h] = _shifted_bias(r - c + tq, rb_ref, h)
            bias_ref[2, h] = _shifted_bias(r - c, rb_ref, h)

    for h in range(N_HEADS):
        keep = lo_half if h % 2 == 0 else jnp.logical_not(lo_half)
        pr = h // 2
        qm_ref[h] = jnp.where(keep, q_ref[:, pr * LANES:(pr + 1) * LANES].astype(F32), 0.0).astype(BF16)
        qim_ref[h] = jnp.where(keep, qi_ref[:, pr * LANES:(pr + 1) * LANES].astype(F32), 0.0).astype(BF16)

    wcols = [ws_ref[:, h:h + 1] for h in range(N_IDX_HEADS)]

    def score_chunk(c, carry):
        rmin, rmax = carry
        off = pl.multiple_of(c * kb, kb)
        kt = kit_ref[:, pl.ds(off, kb)]
        acc = jnp.zeros((tq, kb), F32)
        for h in range(N_IDX_HEADS):
            s = jnp.dot(qim_ref[h], kt, preferred_element_type=F32)
            acc = acc + jnp.maximum(s, 0.0) * wcols[h]
        key_pos = off + lax.broadcasted_iota(jnp.int32, (tq, kb), 1)
        valid = key_pos <= q_pos
        sc_ref[:, pl.ds(off, kb)] = jnp.where(valid, acc, NEG_INF)
        rmax = jnp.maximum(rmax, _lane_tile_max(jnp.where(valid, acc, NEG_INF)))
        rmin = jnp.minimum(rmin, _lane_tile_min(jnp.where(valid, acc, -NEG_INF)))
        return rmin, rmax

    rmin, rmax = lax.fori_loop(0, nch, score_chunk,
                               (jnp.full((tq, LANES), -NEG_INF, F32), jnp.full((tq, LANES), NEG_INF, F32)))
    rmin = jnp.min(rmin, axis=-1, keepdims=True)
    rmax = jnp.max(rmax, axis=-1, keepdims=True)

    topk = min(TOPK_MAX, kit_ref.shape[1] // 4)
    k_row = jnp.minimum(q_pos + 1, topk).astype(F32)
    nscan = ((i + 1) * tq + scan_kb - 1) // scan_kb
    thr = _topk_threshold_two_level(sc_ref, cand_ref, res_ref, nscan, scan_kb, k_row, (q_pos + 1).astype(F32),
                                    rmin, rmax, N_COARSE, n_bisect, n_index_bits)

    thr_row = jnp.concatenate([jnp.broadcast_to(thr, (tq, LANES))] * (scan_kb // LANES), axis=1)

    def to_mask(c, _):
        off = pl.multiple_of(c * scan_kb, scan_kb)
        sc_ref[:, pl.ds(off, scan_kb)] = jnp.where(sc_ref[:, pl.ds(off, scan_kb)] >= thr_row, 0.0, NEG_INF)
        return 0
    lax.fori_loop(0, nscan, to_mask, 0)

    def logits(c, h, off, with_bias):
        pr = h // 2
        kp = kt_ref[pr * LANES:(pr + 1) * LANES, pl.ds(off, kb)]
        s = jnp.dot(qm_ref[h], kp, preferred_element_type=F32) + sc_ref[:, pl.ds(off, kb)]
        if with_bias:
            tiles = [bias_ref[jnp.clip(c * sub + t - i + 2, 0, 2), h] for t in range(sub)]
            s = s + jnp.concatenate(tiles, axis=1)
        return s

    def max_chunk(c, with_bias):
        off = pl.multiple_of(c * kb, kb)
        for h in range(N_HEADS):
            m_ref[h] = jnp.maximum(m_ref[h], _lane_tile_max(logits(c, h, off, with_bias)))

    def acc_chunk(c, with_bias):
        off = pl.multiple_of(c * kb, kb)
        for pr in range(n_pairs):
            vp = vt_ref[pr * LANES:(pr + 1) * LANES, pl.ds(off, kb)]
            for hh in range(2):
                h = 2 * pr + hh
                p = jnp.exp2(logits(c, h, off, with_bias) - jnp.concatenate([m_ref[h]] * sub, axis=1))
                l_ref[h] = l_ref[h] + _lane_tile_sum(p)
                pv = lax.dot_general(p.astype(BF16), vp, _NT, preferred_element_type=F32)
                mine = lo_half if hh == 0 else jnp.logical_not(lo_half)
                acc_ref[pr] = acc_ref[pr] + jnp.where(mine, pv, 0.0)

    def sweep(fn):
        n_far = jnp.maximum((i * tq - MAX_DISTANCE + 1) // kb, 0)

        def far_body(c, _):
            fn(c, False)
            return 0

        def near_body(c, _):
            fn(c, True)
            return 0

        lax.fori_loop(0, n_far, far_body, 0)
        lax.fori_loop(n_far, nch, near_body, 0)

    m_ref[...] = jnp.full(m_ref.shape, NEG_INF, F32)
    sweep(max_chunk)
    for h in range(N_HEADS):
        m_ref[h] = jnp.broadcast_to(jnp.max(m_ref[h], axis=-1, keepdims=True), (tq, LANES))
    l_ref[...] = jnp.zeros(l_ref.shape, F32)
    acc_ref[...] = jnp.zeros(acc_ref.shape, F32)
    sweep(acc_chunk)

    for pr in range(n_pairs):
        l0 = jnp.sum(l_ref[2 * pr], axis=-1, keepdims=True)
        l1 = jnp.sum(l_ref[2 * pr + 1], axis=-1, keepdims=True)
        o_ref[:, pr * LANES:(pr + 1) * LANES] = acc_ref[pr] / jnp.where(lo_half, l0, l1)


def _pattn_call(rel_bias, q, qi, ws, kit, kt16, vt16, tq, kb, scan_kb):
    bsz, s, d_attn = q.shape
    assert tq == LANES and MAX_DISTANCE <= tq and s % kb == 0 and kb % scan_kb == 0 and scan_kb % tq == 0
    n_index_bits = int(math.ceil(math.log2(s))) + 1
    blk = lambda c: pl.BlockSpec((None, tq, c), lambda b, i: (b, i, 0))
    whole = lambda r, c: pl.BlockSpec((None, r, c), lambda b, i: (b, 0, 0), pipeline_mode=pl.Buffered(1))
    return pl.pallas_call(
        functools.partial(_pattn_kernel, tq=tq, kb=kb, scan_kb=scan_kb, n_bisect=N_BISECT,
                          n_index_bits=n_index_bits),
        grid=(bsz, s // tq),
        in_specs=[pl.BlockSpec(memory_space=pltpu.SMEM),
                  blk(d_attn), blk(N_IDX_HEADS * IDX_DIM), blk(LANES),
                  whole(2 * IDX_DIM, s), whole(d_attn, s), whole(d_attn, s)],
        out_specs=blk(d_attn),
        out_shape=jax.ShapeDtypeStruct((bsz, s, d_attn), F32),
        scratch_shapes=[
            pltpu.VMEM((tq, s), F32),
            pltpu.VMEM((tq, 3 * LANES), F32),
            pltpu.VMEM((tq, 2 * LANES), F32),
            pltpu.VMEM((N_HEADS, tq, LANES), BF16),
            pltpu.VMEM((N_IDX_HEADS, tq, LANES), BF16),
            pltpu.VMEM((3, N_HEADS, tq, tq), F32),
            pltpu.VMEM((N_HEADS, tq, LANES), F32),
            pltpu.VMEM((N_HEADS, tq, LANES), F32),
            pltpu.VMEM((N_HEADS // 2, tq, LANES), F32),
        ],
        compiler_params=_cparams(2),
        name="pattn",
    )(rel_bias, q, qi, ws, kit, kt16, vt16)


def _sattn_kernel(pt_ref, rb_ref, q_ref, qis_ref, w_ref, kin_ref, kn_ref, vn_ref,
                  cki_hbm, ck_hbm, cv_hbm, o_ref,
                  sc_ref, ibuf, kbuf, vbuf, isem, ksem, vsem, *, cp, page, n_bisect, n_index_bits):
    b = pl.program_id(0)
    n_pages = pt_ref.shape[1]
    nc = n_pages // cp
    ck = cp * page
    past = n_pages * page
    sn = q_ref.shape[0]
    rows = N_HEADS * sn
    d_attn = q_ref.shape[1]

    n_b = pl.num_programs(0)
    i_slots, kv_slots = ibuf.shape[0], kbuf.shape[0]

    def page_copies(cache, buf, sem, seq):
        row, c, slot = b + seq // nc, seq % nc, seq % buf.shape[0]
        return [pltpu.make_async_copy(cache.at[pt_ref[row, c * cp + j]],
                                      buf.at[slot, :, pl.ds(j * page, page)], sem.at[slot])
                for j in range(cp)]

    def start(streams, seq):
        def go():
            for cache, buf, sem in streams:
                for cpy in page_copies(cache, buf, sem, seq):
                    cpy.start()
        if seq < nc:
            go()
        else:
            pl.when(b + 1 < n_b)(go)

    def wait(streams, seq):
        for cache, buf, sem in streams:
            for cpy in page_copies(cache, buf, sem, seq):
                cpy.wait()

    idx_stream = [(cki_hbm, ibuf, isem)]
    kv_streams = [(ck_hbm, kbuf, ksem), (cv_hbm, vbuf, vsem)]
    i_ahead, kv_ahead = i_slots - 1, kv_slots - 1

    @pl.when(b == 0)
    def _():
        for seq in range(i_ahead):
            start(idx_stream, seq)
        for seq in range(kv_ahead):
            start(kv_streams, seq)

    q_pos = past + lax.broadcasted_iota(jnp.int32, (sn, 1), 0)
    qis = qis_ref[...]
    w = w_ref[...]

    def head_sum(s):
        acc = jnp.zeros((sn, s.shape[1]), F32)
        for h in range(N_IDX_HEADS):
            acc = acc + jnp.maximum(s[h * sn:(h + 1) * sn], 0.0) * w[h * sn:(h + 1) * sn]
        return acc

    rmin = jnp.full((sn, LANES), -NEG_INF, F32)
    rmax = jnp.full((sn, LANES), NEG_INF, F32)
    for c in range(nc):
        slot = c % i_slots
        start(idx_stream, c + i_ahead)
        wait(idx_stream, c)
        acc = head_sum(jnp.dot(qis, ibuf[slot].astype(BF16), preferred_element_type=F32))
        sc_ref[:, c * ck:(c + 1) * ck] = acc
        rmax = jnp.maximum(rmax, _lane_tile_max(acc))
        rmin = jnp.minimum(rmin, _lane_tile_min(acc))
    acc = head_sum(jnp.dot(qis, kin_ref[...], preferred_element_type=F32))
    lane = lax.broadcasted_iota(jnp.int32, (sn, LANES), 1)
    valid = jnp.logical_and(lane < sn, past + lane <= q_pos)
    sc_ref[:, past:past + LANES] = jnp.where(valid, acc, NEG_INF)
    sc_ref[:, past + LANES:past + ck] = jnp.full((sn, ck - LANES), NEG_INF, F32)
    rmax = jnp.max(jnp.maximum(rmax, jnp.where(valid, acc, NEG_INF)), axis=-1, keepdims=True)
    rmin = jnp.min(jnp.minimum(rmin, jnp.where(valid, acc, -NEG_INF)), axis=-1, keepdims=True)

    topk = min(TOPK_MAX, (past + sn) // 4)
    k_row = jnp.minimum(q_pos + 1, topk).astype(F32)
    thr = _topk_threshold(sc_ref, nc + 1, ck, k_row, rmin, rmax, n_bisect, n_index_bits)

    r_head = lax.broadcasted_iota(jnp.int32, (rows, d_attn), 0) // sn
    c_head = lax.broadcasted_iota(jnp.int32, (rows, d_attn), 1) // HEAD_DIM
    own = r_head == c_head
    qbd = jnp.where(own, jnp.tile(q_ref[...].astype(F32), (N_HEADS, 1)), 0.0).astype(BF16)

    def softmax_step(state, s, vals_t):
        m_prev, l_prev, acc_prev = state
        m_next = jnp.maximum(m_prev, jnp.max(s, axis=-1, keepdims=True))
        alpha = jnp.exp2(m_prev - m_next)
        p = jnp.exp2(s - m_next[:, :1])
        l_next = alpha * l_prev + _lane_tile_sum(p)
        pv = lax.dot_general(p.astype(BF16), vals_t, _NT, preferred_element_type=F32)
        return m_next, l_next, acc_prev * alpha[:, :1] + pv

    def biased(s, key0):
        dist = q_pos - (key0 + lax.broadcasted_iota(jnp.int32, (sn, s.shape[1]), 1))
        return s + jnp.concatenate([_shifted_bias(dist, rb_ref, h) for h in range(N_HEADS)], axis=0)

    state = (jnp.full((rows, LANES), NEG_INF, F32), jnp.zeros((rows, LANES), F32),
             jnp.zeros((rows, d_attn), F32))
    for c in range(nc):
        slot = c % kv_slots
        start(kv_streams, c + kv_ahead)
        wait(kv_streams, c)
        madd = jnp.where(sc_ref[:, c * ck:(c + 1) * ck] >= thr, 0.0, NEG_INF)
        s = jnp.dot(qbd, kbuf[slot].astype(BF16), preferred_element_type=F32)
        s = s + jnp.tile(madd, (N_HEADS, 1))
        if (c + 1) * ck + MAX_DISTANCE > past:
            s = biased(s, c * ck)
        state = softmax_step(state, s, vbuf[slot].astype(BF16))
    madd = jnp.where(sc_ref[:, past:past + LANES] >= thr, 0.0, NEG_INF)
    s = jnp.dot(qbd, kn_ref[...], preferred_element_type=F32)
    s = biased(s + jnp.tile(madd, (N_HEADS, 1)), past)
    _, l_fin, acc = softmax_step(state, s, vn_ref[...])
    res = jnp.where(own, acc / jnp.sum(l_fin, axis=-1, keepdims=True), 0.0)
    out = res[0:sn]
    for h in range(1, N_HEADS):
        out = out + res[h * sn:(h + 1) * sn]
    o_ref[...] = out


def _sattn_call(page_table, rel_bias, q, qis, w, kin_t, kn_t, vn_t, cache_kidx_t, cache_k_t, cache_v_t, cp):
    db, sn, d_attn = q.shape
    n_pages = page_table.shape[1]
    page = cache_k_t.shape[2]
    assert n_pages % cp == 0 and sn == 8 and page == LANES and cp >= 2 and MAX_DISTANCE <= cp * page
    ck = cp * page
    rows = N_HEADS * sn
    nc = n_pages // cp
    kv_slots = math.gcd(nc, KV_SLOTS)
    assert nc % 2 == 0 and kv_slots >= 2
    n_index_bits = int(math.ceil(math.log2(n_pages * page + ck))) + 1
    per_b = lambda r, c: pl.BlockSpec((None, r, c), lambda b, pt: (b, 0, 0))
    grid_spec = pltpu.PrefetchScalarGridSpec(
        num_scalar_prefetch=1,
        grid=(db,),
        in_specs=[pl.BlockSpec(memory_space=pltpu.SMEM),
                  per_b(sn, d_attn), per_b(rows, IDX_DIM), per_b(rows, 1),
                  per_b(IDX_DIM, LANES), per_b(d_attn, LANES), per_b(d_attn, LANES),
                  pl.BlockSpec(memory_space=pl.ANY), pl.BlockSpec(memory_space=pl.ANY),
                  pl.BlockSpec(memory_space=pl.ANY)],
        out_specs=per_b(sn, d_attn),
        scratch_shapes=[
            pltpu.VMEM((sn, n_pages * page + ck), F32),
            pltpu.VMEM((2, IDX_DIM, ck), F32),
            pltpu.VMEM((kv_slots, d_attn, ck), F32),
            pltpu.VMEM((kv_slots, d_attn, ck), F32),
            pltpu.SemaphoreType.DMA((2,)), pltpu.SemaphoreType.DMA((kv_slots,)), pltpu.SemaphoreType.DMA((kv_slots,)),
        ],
    )
    return pl.pallas_call(
        functools.partial(_sattn_kernel, cp=cp, page=page, n_bisect=N_BISECT, n_index_bits=n_index_bits),
        grid_spec=grid_spec,
        out_shape=jax.ShapeDtypeStruct((db, sn, d_attn), F32),
        compiler_params=_cparams(1),
        name="sattn",
    )(page_table, rel_bias, q, qis, w, kin_t, kn_t, vn_t, cache_kidx_t, cache_k_t, cache_v_t)


def _gelu_tanh(x):
    return 0.5 * x * (1.0 + jnp.tanh(math.sqrt(2.0 / math.pi) * (x + 0.044715 * x * x * x)))


def _ssm_kernel(u_ref, h0r_ref, h0i_ref, lr_ref, li_ref, ldt_ref, bre_ref, bim_ref, cre_ref, cim_ref,
                d_ref, wglu_ref, o_ref, hr_out_ref, hi_out_ref,
                lbr_ref, lbi_ref, bbr_ref, bbi_ref, bur_ref, bui_ref, hre_ref, him_ref, sr_ref, si_ref,
                *, nb, unroll):
    step = pl.program_id(0)
    tt = u_ref.shape[0]
    d_ssm = u_ref.shape[1]

    @pl.when(step == 0)
    def _():
        lr, li = lr_ref[...], li_ref[...]
        dt = jnp.exp(ldt_ref[...])
        mag = jnp.exp(lr * dt)
        lbr, lbi = mag * jnp.cos(li * dt), mag * jnp.sin(li * dt)
        den = lr * lr + li * li
        xr, xi = lbr - 1.0, lbi
        cr = (xr * lr + xi * li) / den
        ci = (xi * lr - xr * li) / den
        lbr_ref[...] = lbr
        lbi_ref[...] = lbi
        bbr_ref[...] = (cr * bre_ref[...] - ci * bim_ref[...]).astype(BF16)
        bbi_ref[...] = (cr * bim_ref[...] + ci * bre_ref[...]).astype(BF16)
        sr_ref[...] = h0r_ref[...]
        si_ref[...] = h0i_ref[...]

    u = u_ref[...]
    u16 = u.astype(BF16)
    bur_ref[...] = jnp.dot(u16, bbr_ref[...], preferred_element_type=F32)
    bui_ref[...] = jnp.dot(u16, bbi_ref[...], preferred_element_type=F32)
    lbr = jnp.broadcast_to(lbr_ref[...], (nb, lbr_ref.shape[1]))
    lbi = jnp.broadcast_to(lbi_ref[...], (nb, lbi_ref.shape[1]))

    def scan(t, carry):
        hr, hi = carry
        r0 = pl.multiple_of(t * nb, nb)
        nr = lbr * hr - lbi * hi + bur_ref[pl.ds(r0, nb), :]
        ni = lbr * hi + lbi * hr + bui_ref[pl.ds(r0, nb), :]
        hre_ref[pl.ds(r0, nb), :] = nr
        him_ref[pl.ds(r0, nb), :] = ni
        return nr, ni

    hr, hi = lax.fori_loop(0, tt // nb, scan, (sr_ref[...], si_ref[...]), unroll=unroll)
    sr_ref[...] = hr
    si_ref[...] = hi
    hr_out_ref[...] = hr
    hi_out_ref[...] = hi

    y = (jnp.dot(hre_ref[...].astype(BF16), cre_ref[...], preferred_element_type=F32)
         - jnp.dot(him_ref[...].astype(BF16), cim_ref[...], preferred_element_type=F32)
         + d_ref[...] * u)
    z = _gelu_tanh(y)
    g = jnp.dot(z.astype(BF16), wglu_ref[...], preferred_element_type=F32)
    o_ref[...] = g[:, :d_ssm] / (1.0 + jnp.exp(-g[:, d_ssm:]))


def _ssm_call(u, h0r, h0i, lr, li, ldt, bre, bim, cre, cim, d, wglu, nb, tt):
    n, d_ssm = u.shape
    ns = lr.shape[1]
    assert n % tt == 0 and tt % nb == 0
    unroll = 8 if (tt // nb) % 8 == 0 else 1
    state = jax.ShapeDtypeStruct((nb, ns), F32)
    return pl.pallas_call(
        functools.partial(_ssm_kernel, nb=nb, unroll=unroll),
        grid=(n // tt,),
        in_specs=[pl.BlockSpec((tt, d_ssm), lambda i: (i, 0)),
                  _resident((nb, ns)), _resident((nb, ns)),
                  _resident((1, ns)), _resident((1, ns)), _resident((1, ns)),
                  _resident((d_ssm, ns)), _resident((d_ssm, ns)),
                  _resident((ns, d_ssm)), _resident((ns, d_ssm)),
                  _resident((1, d_ssm)), _resident((d_ssm, 2 * d_ssm))],
        out_specs=(pl.BlockSpec((tt, d_ssm), lambda i: (i, 0)),
                   pl.BlockSpec((nb, ns), lambda i: (0, 0)), pl.BlockSpec((nb, ns), lambda i: (0, 0))),
        out_shape=(jax.ShapeDtypeStruct((n, d_ssm), F32), state, state),
        scratch_shapes=[
            pltpu.VMEM((1, ns), F32), pltpu.VMEM((1, ns), F32),
            pltpu.VMEM((d_ssm, ns), BF16), pltpu.VMEM((d_ssm, ns), BF16),
            pltpu.VMEM((tt, ns), F32), pltpu.VMEM((tt, ns), F32),
            pltpu.VMEM((tt, ns), F32), pltpu.VMEM((tt, ns), F32),
            pltpu.VMEM((nb, ns), F32), pltpu.VMEM((nb, ns), F32),
        ],
        compiler_params=_cparams(1),
        name="ssm",
    )(u, h0r, h0i, lr, li, ldt, bre, bim, cre, cim, d, wglu)


def _finish_kernel(x_ref, attn_ref, ssm_ref, gi_ref, bi_ref, wo_ref, g1_ref, b1_ref, wr_ref, br_ref,
                   wg_ref, wu_ref, wd_ref, g2_ref, b2_ref, o_ref, *, alpha, pre_ln):
    x = x_ref[...]
    if pre_ln:
        x = _layer_norm(x, gi_ref[...], bi_ref[...])
    d_attn = attn_ref.shape[1]
    mix = (jnp.dot(attn_ref[...].astype(BF16), wo_ref[:d_attn, :], preferred_element_type=F32)
           + jnp.dot(ssm_ref[...].astype(BF16), wo_ref[d_attn:, :], preferred_element_type=F32))
    x1 = _layer_norm(alpha * x + mix, g1_ref[...], b1_ref[...])
    x16 = x1.astype(BF16)

    lg = jnp.dot(x16, wr_ref[...], preferred_element_type=F32) + br_ref[...]
    lane = lax.broadcasted_iota(jnp.int32, lg.shape, 1)
    big = jnp.int32(1 << 20)
    is_grp = lane < N_EXPERT_GROUPS
    gl = jnp.where(is_grp, lg, NEG_INF)
    gmax = jnp.max(gl, axis=-1, keepdims=True)
    g_sel = jnp.min(jnp.where(gl == gmax, lane, big), axis=-1, keepdims=True)
    p_grp = 1.0 / jnp.sum(jnp.where(is_grp, jnp.exp(gl - gmax), 0.0), axis=-1, keepdims=True)
    e_idx = lane - N_EXPERT_GROUPS
    in_sel = jnp.logical_and(jnp.logical_and(e_idx >= 0, e_idx < N_EXPERTS),
                             e_idx // EXPERTS_PER_GROUP == g_sel)
    el = jnp.where(in_sel, lg, NEG_INF)
    v1 = jnp.max(el, axis=-1, keepdims=True)
    i1 = jnp.min(jnp.where(el == v1, lane, big), axis=-1, keepdims=True)
    el2 = jnp.where(lane == i1, NEG_INF, el)
    v2 = jnp.max(el2, axis=-1, keepdims=True)
    i2 = jnp.min(jnp.where(el2 == v2, lane, big), axis=-1, keepdims=True)
    e21 = jnp.exp(v2 - v1)
    w1 = p_grp / (1.0 + e21)
    w2 = p_grp * e21 / (1.0 + e21)
    gates = jnp.where(lane == i1, w1, jnp.where(lane == i2, w2, 0.0))

    y = jnp.zeros(x1.shape, F32)
    for e in range(N_EXPERTS):
        gate = gates[:, N_EXPERT_GROUPS + e:N_EXPERT_GROUPS + e + 1]
        hg = jnp.dot(x16, wg_ref[e], preferred_element_type=F32)
        hu = jnp.dot(x16, wu_ref[e], preferred_element_type=F32)
        hid = hg / (1.0 + jnp.exp(-hg)) * hu * gate
        y = y + jnp.dot(hid.astype(BF16), wd_ref[e], preferred_element_type=F32)
    o_ref[...] = _layer_norm(alpha * x1 + y, g2_ref[...], b2_ref[...])


def _finish_call(x, attn, ssm, gi, bi, wo, g1, b1, wr, br, wg, wu, wd, g2, b2, alpha, pre_ln, tm):
    n, d = x.shape
    row = lambda c: pl.BlockSpec((tm, c), lambda i: (i, 0))
    return pl.pallas_call(
        functools.partial(_finish_kernel, alpha=alpha, pre_ln=pre_ln),
        grid=(n // tm,),
        in_specs=[row(d), row(attn.shape[1]), row(ssm.shape[1]),
                  _resident(gi.shape), _resident(bi.shape), _resident(wo.shape),
                  _resident(g1.shape), _resident(b1.shape), _resident(wr.shape), _resident(br.shape),
                  _resident(wg.shape), _resident(wu.shape), _resident(wd.shape),
                  _resident(g2.shape), _resident(b2.shape)],
        out_specs=row(d),
        out_shape=jax.ShapeDtypeStruct((n, d), F32),
        compiler_params=_cparams(1),
        name="finish",
    )(x, attn, ssm, gi, bi, wo, g1, b1, wr, br, wg, wu, wd, g2, b2)


def _row(a):
    return a.reshape(1, -1).astype(F32)


def _block_diag(a):
    g, r, c = a.shape
    return jnp.einsum("grc,gh->grhc", a, jnp.eye(g, dtype=a.dtype)).reshape(g * r, g * c)


def _pad_lanes(a):
    return jnp.pad(a, ((0, 0), (0, 0), (0, LANES - a.shape[2])))


def _pick_tile(n, pref):
    t = min(n, pref)
    while n % t:
        t //= 2
    return t


def kernel(x_prompt, x_sample, cache_k, cache_v, cache_kidx, state_ssm_re, state_ssm_im, page_table, ln_in_g, ln_in_b, rel_bias, w_in, idx_ln_g, idx_ln_b, ssm_lam_re, ssm_lam_im, ssm_log_dt, ssm_b_re, ssm_b_im, ssm_c_re, ssm_c_im, ssm_d, w_glu, w_out, ln_mix_g, ln_mix_b, w_router_grp, b_router_grp, w_router_exp, b_router_exp, w_gate, w_up, w_down, ln_ffn_g, ln_ffn_b):
    bsz, seq, d_model = x_prompt.shape
    dbsz, dseq, _ = x_sample.shape
    depth = w_in.shape[0]
    d_attn = N_HEADS * HEAD_DIM
    n_grp = ssm_lam_re.shape[1]
    d_ssm = n_grp * SSM_GROUP
    n_state = n_grp * SSM_STATE
    n_pool, page = cache_k.shape[1], cache_k.shape[2]
    alpha = (2.0 * depth) ** 0.25
    sizes = (d_attn, d_attn, d_attn, N_IDX_HEADS * IDX_DIM, IDX_DIM, N_IDX_HEADS, d_ssm)
    cuts = [0] + [int(c) for c in np.cumsum(sizes)]
    rel_bias = rel_bias.astype(F32)
    page_table = page_table.astype(jnp.int32)

    xp = x_prompt.reshape(bsz * seq, d_model)
    xs = x_sample.reshape(dbsz * dseq, d_model)
    outs = [[] for _ in range(10)]
    for l in range(depth):
        pre_ln = l == 0
        seg = [w_in[l][:, cuts[j]:cuts[j + 1]] for j in range(7)]
        pad = jnp.zeros((d_model, LANES - N_IDX_HEADS), w_in.dtype)
        w_row = jnp.concatenate([seg[0], seg[3], seg[6], seg[5], pad], axis=1).astype(BF16)
        w_t = jnp.concatenate([seg[1], seg[2], seg[4]], axis=1).T.astype(BF16)
        proj_w = (_row(ln_in_g), _row(ln_in_b), w_row, w_t,
                  idx_ln_g[l].reshape(IDX_DIM, 1).astype(F32), idx_ln_b[l].reshape(IDX_DIM, 1).astype(F32))
        ssm_w = (_row(ssm_lam_re[l]), _row(ssm_lam_im[l]), _row(jnp.repeat(ssm_log_dt[l], SSM_STATE)),
                 _block_diag(jnp.swapaxes(ssm_b_re[l], 1, 2)).astype(F32),
                 _block_diag(jnp.swapaxes(ssm_b_im[l], 1, 2)).astype(F32),
                 _block_diag(jnp.swapaxes(ssm_c_re[l], 1, 2)).astype(BF16),
                 _block_diag(jnp.swapaxes(ssm_c_im[l], 1, 2)).astype(BF16),
                 _row(ssm_d[l]), w_glu[l].astype(BF16))
        wr = jnp.concatenate([w_router_grp[l], jnp.transpose(w_router_exp[l], (1, 0, 2)).reshape(d_model, N_EXPERTS)], axis=1)
        wr = jnp.pad(wr, ((0, 0), (0, LANES - wr.shape[1]))).astype(BF16)
        br = jnp.pad(jnp.concatenate([b_router_grp[l], b_router_exp[l].reshape(-1)]),
                     (0, LANES - N_EXPERT_GROUPS - N_EXPERTS)).reshape(1, LANES).astype(F32)
        fin_w = (_row(ln_in_g), _row(ln_in_b), w_out[l].astype(BF16), _row(ln_mix_g[l]), _row(ln_mix_b[l]), wr, br,
                 w_gate[l].astype(BF16), w_up[l].astype(BF16), w_down[l].astype(BF16),
                 _row(ln_ffn_g[l]), _row(ln_ffn_b[l]))

        attn_l, k_l, v_l, ki_l, u_l = [], [], [], [], []
        for b in range(bsz):
            q, qi, u, ws, kt, vt, kt16, vt16, kit, kit16 = _proj_call(
                xp[b * seq:(b + 1) * seq], *proj_w, d_attn, d_ssm, pre_ln, _pick_tile(seq, PROJ_ROWS))
            attn_l.append(_pattn_call(rel_bias, q[None], qi[None], ws[None], kit16[None], kt16[None], vt16[None],
                                      tq=QUERY_BLOCK, kb=KEY_CHUNK, scan_kb=SCAN_CHUNK)[0])
            k_l.append(jnp.transpose(kt.reshape(N_HEADS, HEAD_DIM, seq), (2, 0, 1)))
            v_l.append(jnp.transpose(vt.reshape(N_HEADS, HEAD_DIM, seq), (2, 0, 1)))
            ki_l.append(kit.T)
            u_l.append(u)
        u_tm = jnp.stack(u_l, axis=1).reshape(seq * bsz, d_ssm)
        h0 = jnp.zeros((bsz, n_state), F32)
        ssm_tm, hr, hi = _ssm_call(u_tm, h0, h0, *ssm_w, nb=bsz, tt=_pick_tile(seq, SSM_ROWS) * bsz)
        ssm_o = jnp.swapaxes(ssm_tm.reshape(seq, bsz, d_ssm), 0, 1).reshape(bsz * seq, d_ssm)
        xp = _finish_call(xp, jnp.concatenate(attn_l, axis=0), ssm_o, *fin_w, alpha=alpha, pre_ln=pre_ln,
                          tm=_pick_tile(bsz * seq, FINISH_ROWS))
        outs[0].append(jnp.stack(k_l))
        outs[1].append(jnp.stack(v_l))
        outs[2].append(jnp.stack(ki_l))
        outs[3].append(hr.reshape(bsz, n_grp, SSM_STATE).astype(state_ssm_re.dtype))
        outs[4].append(hi.reshape(bsz, n_grp, SSM_STATE).astype(state_ssm_im.dtype))

        n_s = dbsz * dseq
        q, qi, u, ws, kt, vt, kt16, vt16, kit, kit16 = _proj_call(xs, *proj_w, d_attn, d_ssm, pre_ln,
                                                                  _pick_tile(n_s, PROJ_ROWS))
        rows = N_IDX_HEADS * dseq
        qis = jnp.swapaxes(qi.reshape(dbsz, dseq, N_IDX_HEADS, IDX_DIM), 1, 2).reshape(dbsz, rows, IDX_DIM)
        wis = jnp.swapaxes(ws[:, :N_IDX_HEADS].reshape(dbsz, dseq, N_IDX_HEADS), 1, 2).reshape(dbsz, rows, 1)
        per_seq_t = lambda a: _pad_lanes(jnp.swapaxes(a.reshape(a.shape[0], dbsz, dseq), 0, 1))
        ck_t = jnp.transpose(cache_k[l], (0, 2, 3, 1)).reshape(n_pool, d_attn, page)
        cv_t = jnp.transpose(cache_v[l], (0, 2, 3, 1)).reshape(n_pool, d_attn, page)
        cki_t = jnp.swapaxes(cache_kidx[l], 1, 2)
        attn = _sattn_call(page_table, rel_bias, q.reshape(dbsz, dseq, d_attn), qis, wis,
                           per_seq_t(kit16[:IDX_DIM]), per_seq_t(kt16), per_seq_t(vt16), cki_t, ck_t, cv_t,
                           cp=min(PAGES_PER_CHUNK, page_table.shape[1]))
        u_tm = jnp.swapaxes(u.reshape(dbsz, dseq, d_ssm), 0, 1).reshape(n_s, d_ssm)
        ssm_tm, hr, hi = _ssm_call(u_tm, state_ssm_re[l].reshape(dbsz, n_state).astype(F32),
                                   state_ssm_im[l].reshape(dbsz, n_state).astype(F32), *ssm_w, nb=dbsz, tt=n_s)
        ssm_o = jnp.swapaxes(ssm_tm.reshape(dseq, dbsz, d_ssm), 0, 1).reshape(n_s, d_ssm)
        xs = _finish_call(xs, attn.reshape(n_s, d_attn), ssm_o, *fin_w, alpha=alpha, pre_ln=pre_ln,
                          tm=_pick_tile(n_s, FINISH_ROWS))
        outs[5].append(kt.T.reshape(dbsz, dseq, N_HEADS, HEAD_DIM))
        outs[6].append(vt.T.reshape(dbsz, dseq, N_HEADS, HEAD_DIM))
        outs[7].append(kit.T.reshape(dbsz, dseq, IDX_DIM))
        outs[8].append(hr.reshape(dbsz, n_grp, SSM_STATE).astype(state_ssm_re.dtype))
        outs[9].append(hi.reshape(dbsz, n_grp, SSM_STATE).astype(state_ssm_im.dtype))

    return (xp.reshape(bsz, seq, d_model), xs.reshape(dbsz, dseq, d_model)) + tuple(jnp.stack(o) for o in outs)
```

```python
import functools
import math

import numpy as np
import jax
import jax.numpy as jnp
from jax import lax
from jax.experimental import pallas as pl
from jax.experimental.pallas import tpu as pltpu

N_HEADS = 8
HEAD_DIM = 64
N_IDX_HEADS = 8
IDX_DIM = 64
TOPK_MAX = 256
NUM_BUCKETS = 32
MAX_DISTANCE = 128
SSM_GROUP = 16
SSM_STATE = 64
N_EXPERT_GROUPS = 4
EXPERTS_PER_GROUP = 4
N_EXPERTS = N_EXPERT_GROUPS * EXPERTS_PER_GROUP
LN_EPS = 1e-5
NEG_INF = -1e30
LOG2E = math.log2(math.e)

LANES = 128
VMEM_LIMIT = 56 * 1024 * 1024

PROJ_ROWS = 512
FINISH_ROWS = 512
SSM_ROWS = 512
QUERY_BLOCK = LANES
KEY_CHUNK = 1024
SCAN_CHUNK = 512
PAGES_PER_CHUNK = 16
KV_SLOTS = 4
N_BISECT = 16
N_COARSE = 10
N_FINE = 8
SHIFT_UNDERFLOW = 2.0 ** -100

F32 = jnp.float32
BF16 = jnp.bfloat16
_NT = (((1,), (1,)), ((), ()))


def _cparams(n_grid, vmem=VMEM_LIMIT):
    return pltpu.CompilerParams(dimension_semantics=("arbitrary",) * n_grid, vmem_limit_bytes=vmem)


def _resident(shape):
    nd = len(shape)
    return pl.BlockSpec(shape, lambda *_: (0,) * nd, pipeline_mode=pl.Buffered(1))


def _layer_norm(x, g, b):
    mu = jnp.mean(x, axis=-1, keepdims=True)
    xc = x - mu
    var = jnp.mean(xc * xc, axis=-1, keepdims=True)
    return xc * lax.rsqrt(var + LN_EPS) * g + b


def _proj_kernel(x_ref, g_ref, b_ref, wr_ref, wt_ref, ig_ref, ib_ref,
                 q_ref, qi_ref, u_ref, ws_ref, kt_ref, vt_ref, ktb_ref, vtb_ref, kit_ref, kitb_ref,
                 *, d_attn, d_ssm, pre_ln):
    xn = x_ref[...]
    if pre_ln:
        xn = _layer_norm(xn, g_ref[...], b_ref[...])
    x16 = xn.astype(BF16)
    d_qi = N_IDX_HEADS * IDX_DIM
    hr = jnp.dot(x16, wr_ref[...], preferred_element_type=F32)
    ht = lax.dot_general(wt_ref[...], x16, _NT, preferred_element_type=F32)
    q_ref[...] = (hr[:, :d_attn] * (HEAD_DIM ** -0.5 * LOG2E)).astype(BF16)
    qi_ref[...] = (hr[:, d_attn:d_attn + d_qi] * IDX_DIM ** -0.5).astype(BF16)
    u_ref[...] = hr[:, d_attn + d_qi:d_attn + d_qi + d_ssm]
    ws_ref[...] = hr[:, d_attn + d_qi + d_ssm:] * N_IDX_HEADS ** -0.5
    kt = ht[:d_attn]
    vt = ht[d_attn:2 * d_attn]
    kt_ref[...] = kt
    vt_ref[...] = vt
    ktb_ref[...] = kt.astype(BF16)
    vtb_ref[...] = vt.astype(BF16)
    ki = ht[2 * d_attn:]
    mu = jnp.mean(ki, axis=0, keepdims=True)
    kc = ki - mu
    var = jnp.mean(kc * kc, axis=0, keepdims=True)
    kin = kc * lax.rsqrt(var + LN_EPS) * ig_ref[...] + ib_ref[...]
    kit_ref[...] = kin
    kin16 = kin.astype(BF16)
    kitb_ref[...] = jnp.concatenate([kin16, kin16], axis=0)


def _proj_call(x, ln_g, ln_b, w_row, w_t, ig, ib, d_attn, d_ssm, pre_ln, tm):
    n, d = x.shape
    d_qi = N_IDX_HEADS * IDX_DIM
    row = lambda c, dt: (jax.ShapeDtypeStruct((n, c), dt), pl.BlockSpec((tm, c), lambda i: (i, 0)))
    col = lambda r, dt: (jax.ShapeDtypeStruct((r, n), dt), pl.BlockSpec((r, tm), lambda i: (0, i)))
    outs = [row(d_attn, BF16), row(d_qi, BF16), row(d_ssm, F32), row(LANES, F32),
            col(d_attn, F32), col(d_attn, F32), col(d_attn, BF16), col(d_attn, BF16),
            col(IDX_DIM, F32), col(2 * IDX_DIM, BF16)]
    return pl.pallas_call(
        functools.partial(_proj_kernel, d_attn=d_attn, d_ssm=d_ssm, pre_ln=pre_ln),
        grid=(n // tm,),
        in_specs=[pl.BlockSpec((tm, d), lambda i: (i, 0)), _resident((1, d)), _resident((1, d)),
                  _resident(w_row.shape), _resident(w_t.shape), _resident((IDX_DIM, 1)), _resident((IDX_DIM, 1))],
        out_specs=tuple(o[1] for o in outs),
        out_shape=tuple(o[0] for o in outs),
        compiler_params=_cparams(1),
        name="proj",
    )(x, ln_g, ln_b, w_row, w_t, ig, ib)


def _rel_bucket(dist):
    n = jnp.maximum(dist, 0)
    max_exact = NUM_BUCKETS // 2
    scaled = jnp.log(jnp.maximum(n, 1).astype(F32) / max_exact) / math.log(MAX_DISTANCE / max_exact)
    large = jnp.minimum(max_exact + jnp.floor(scaled * (NUM_BUCKETS - max_exact)).astype(jnp.int32),
                        NUM_BUCKETS - 1)
    return jnp.where(n < max_exact, n, large)


def _shifted_bias(dist, rb_ref, h):
    bkt = _rel_bucket(dist)
    far = rb_ref[NUM_BUCKETS - 1, h]
    out = jnp.zeros(dist.shape, F32)
    for b in range(NUM_BUCKETS - 1):
        out = jnp.where(bkt == b, rb_ref[b, h] - far, out)
    return out * LOG2E


def _lane_tile_sum(x):
    acc = x[:, :LANES]
    for t in range(1, x.shape[1] // LANES):
        acc = acc + x[:, t * LANES:(t + 1) * LANES]
    return acc


def _lane_tile_max(x):
    acc = x[:, :LANES]
    for t in range(1, x.shape[1] // LANES):
        acc = jnp.maximum(acc, x[:, t * LANES:(t + 1) * LANES])
    return acc


def _lane_tile_min(x):
    acc = x[:, :LANES]
    for t in range(1, x.shape[1] // LANES):
        acc = jnp.minimum(acc, x[:, t * LANES:(t + 1) * LANES])
    return acc


def _chunk_of(ref, kb):
    return lambda c: ref[:, pl.ds(pl.multiple_of(c * kb, kb), kb)]


def _count_ge(ref, nch, kb, t):
    chunk = _chunk_of(ref, kb)

    def body(c, cnt):
        return cnt + _lane_tile_sum(jnp.where(chunk(c) >= t, 1.0, 0.0))
    cnt = lax.fori_loop(0, nch, body, jnp.zeros((ref.shape[0], LANES), F32))
    return jnp.sum(cnt, axis=-1, keepdims=True)


def _kth_largest(ref, nch, kb, k_row, rmin, rmax, n_bisect):
    rows = ref.shape[0]
    chunk = _chunk_of(ref, kb)

    def bisect(_, lohi):
        lo, hi = lohi
        mid = 0.5 * (lo + hi)
        ok = _count_ge(ref, nch, kb, mid) >= k_row
        return jnp.where(ok, mid, lo), jnp.where(ok, hi, mid)

    _, hi = lax.fori_loop(0, n_bisect, bisect, (rmin, rmax))

    def count_ge_and_next(m):
        def body(c, carry):
            cnt, nxt = carry
            x = chunk(c)
            cnt = cnt + _lane_tile_sum(jnp.where(x >= m, 1.0, 0.0))
            nxt = jnp.maximum(nxt, _lane_tile_max(jnp.where(x < m, x, NEG_INF)))
            return cnt, nxt
        cnt, nxt = lax.fori_loop(0, nch, body, (jnp.zeros((rows, LANES), F32),
                                                jnp.full((rows, LANES), NEG_INF, F32)))
        return jnp.sum(cnt, axis=-1, keepdims=True), jnp.max(nxt, axis=-1, keepdims=True)

    def first_le(t):
        def body(c, nxt):
            x = chunk(c)
            return jnp.maximum(nxt, _lane_tile_max(jnp.where(x <= t, x, NEG_INF)))
        nxt = lax.fori_loop(0, nch, body, jnp.full((rows, LANES), NEG_INF, F32))
        return jnp.max(nxt, axis=-1, keepdims=True)

    def walk_cond(st):
        return st[3] > 0.0

    def walk_body(st):
        m, cfin, done, _ = st
        cnt, nxt = count_ge_and_next(m)
        ok = jnp.logical_and(cnt >= k_row, done < 0.5)
        cfin = jnp.where(ok, cnt, cfin)
        done = jnp.where(cnt >= k_row, 1.0, done)
        m = jnp.where(done > 0.5, m, nxt)
        return m, cfin, done, jnp.sum(1.0 - done)

    zeros = jnp.zeros((rows, 1), F32)
    thr, cfin, _, _ = lax.while_loop(walk_cond, walk_body, (first_le(hi), zeros, zeros, jnp.float32(rows)))
    return thr, cfin


def _drop_ties(ref, nch, kb, thr, k_row, n_index_bits):
    rows = ref.shape[0]
    chunk = _chunk_of(ref, kb)

    def count_gt():
        def body(c, cnt):
            return cnt + _lane_tile_sum(jnp.where(chunk(c) > thr, 1.0, 0.0))
        cnt = lax.fori_loop(0, nch, body, jnp.zeros((rows, LANES), F32))
        return jnp.sum(cnt, axis=-1, keepdims=True)

    need = k_row - count_gt()

    def count_eq_upto(j):
        def body(c, cnt):
            x = chunk(c)
            idx = (c * kb + lax.broadcasted_iota(jnp.int32, x.shape, 1)).astype(F32)
            hit = jnp.where(x == thr, jnp.where(idx <= j, 1.0, 0.0), 0.0)
            return cnt + _lane_tile_sum(hit)
        cnt = lax.fori_loop(0, nch, body, jnp.zeros((rows, LANES), F32))
        return jnp.sum(cnt, axis=-1, keepdims=True)

    def ibisect(_, lohi):
        lo, hi_i = lohi
        mid = jnp.floor(0.5 * (lo + hi_i))
        ok = count_eq_upto(mid) >= need
        return jnp.where(ok, lo, mid), jnp.where(ok, mid, hi_i)

    width = jnp.float32(1.0) * (nch * kb)
    _, cut = lax.fori_loop(0, n_index_bits, ibisect,
                           (jnp.full((rows, 1), -1.0, F32), jnp.zeros((rows, 1), F32) + (width - 1.0)))

    def drop(c, _):
        off = pl.multiple_of(c * kb, kb)
        x = ref[:, pl.ds(off, kb)]
        idx = (c * kb + lax.broadcasted_iota(jnp.int32, x.shape, 1)).astype(F32)
        dropped = jnp.where(x == thr, jnp.where(idx > cut, 1.0, 0.0), 0.0)
        ref[:, pl.ds(off, kb)] = jnp.where(dropped > 0.5, NEG_INF, x)
        return 0
    lax.fori_loop(0, nch, drop, 0)


def _topk_threshold(sc_ref, nch, kb, k_row, rmin, rmax, n_bisect, n_index_bits):
    thr, cfin = _kth_largest(sc_ref, nch, kb, k_row, rmin, rmax, n_bisect)

    @pl.when(jnp.sum(jnp.where(cfin > k_row, 1.0, 0.0)) > 0.0)
    def _():
        _drop_ties(sc_ref, nch, kb, thr, k_row, n_index_bits)

    return thr


def _topk_threshold_two_level(sc_ref, cand_ref, res_ref, nch, kb, k_row, n_valid, rmin, rmax,
                              n_coarse, n_bisect, n_index_bits):
    rows = sc_ref.shape[0]
    chunk = _chunk_of(sc_ref, kb)
    hi0 = rmax + jnp.maximum(jnp.abs(rmax), 1e-30) * 2.0 ** -20

    def bisect(_, st):
        lo, hi, c_lo, c_hi = st
        mid = 0.5 * (lo + hi)
        c = _count_ge(sc_ref, nch, kb, mid)
        ok = c >= k_row
        return (jnp.where(ok, mid, lo), jnp.where(ok, hi, mid), jnp.where(ok, c, c_lo), jnp.where(ok, c_hi, c))

    lo, hi, c_lo, c_hi = lax.fori_loop(0, n_coarse, bisect, (rmin, hi0, n_valid, jnp.zeros((rows, 1), F32)))

    def capture(c, top):
        a, b, d = top
        x = chunk(c)
        x = jnp.where(x < hi, x, NEG_INF)
        for t in range(kb // LANES):
            v = x[:, t * LANES:(t + 1) * LANES]
            a, v = jnp.maximum(a, v), jnp.minimum(a, v)
            b, v = jnp.maximum(b, v), jnp.minimum(b, v)
            d = jnp.maximum(d, v)
        return a, b, d

    neg = jnp.full((rows, LANES), NEG_INF, F32)
    a, b, d = lax.fori_loop(0, nch, capture, (neg, neg, neg))
    cand_ref[:, 0:LANES] = a
    cand_ref[:, LANES:2 * LANES] = b
    cand_ref[:, 2 * LANES:3 * LANES] = d
    kept = _count_ge(cand_ref, 1, 3 * LANES, lo)
    missed = jnp.sum(jnp.where(kept == c_lo - c_hi, 0.0, 1.0))

    @pl.when(missed == 0.0)
    def _():
        thr, cnt = _kth_largest(cand_ref, 1, 3 * LANES, k_row - c_hi, lo, hi, N_FINE)
        res_ref[:, 0:LANES] = jnp.broadcast_to(thr, (rows, LANES))
        res_ref[:, LANES:2 * LANES] = jnp.broadcast_to(cnt + c_hi, (rows, LANES))

    @pl.when(missed > 0.0)
    def _():
        thr, cnt = _kth_largest(sc_ref, nch, kb, k_row, rmin, rmax, n_bisect)
        res_ref[:, 0:LANES] = jnp.broadcast_to(thr, (rows, LANES))
        res_ref[:, LANES:2 * LANES] = jnp.broadcast_to(cnt, (rows, LANES))

    thr = res_ref[:, 0:1]
    cfin = res_ref[:, LANES:LANES + 1]

    @pl.when(jnp.sum(jnp.where(cfin > k_row, 1.0, 0.0)) > 0.0)
    def _():
        _drop_ties(sc_ref, nch, kb, thr, k_row, n_index_bits)

    return thr


def _pattn_kernel(rb_ref, q_ref, qi_ref, ws_ref, kit_ref, kt_ref, vt_ref, o_ref,
                  sc_ref, cand_ref, res_ref, qm_ref, qim_ref, bias_ref, kn_ref, m_ref, l_ref, acc_ref,
                  *, tq, kb, scan_kb, n_bisect, n_index_bits):
    i = pl.program_id(1)
    n_pairs = N_HEADS // 2
    nch = ((i + 1) * tq + kb - 1) // kb
    sub = kb // tq
    lane = lax.broadcasted_iota(jnp.int32, (tq, LANES), 1)
    lo_half = lane < HEAD_DIM
    q_pos = i * tq + lax.broadcasted_iota(jnp.int32, (tq, 1), 0)

    @pl.when(i == 0)
    def _():
        r = lax.broadcasted_iota(jnp.int32, (tq, tq), 0)
        c = lax.broadcasted_iota(jnp.int32, (tq, tq), 1)
        for h in range(N_HEADS):
            bias_ref[0, h] = jnp.zeros((tq, tq), F32)
            bias_ref[1, h] = _shifted_bias(r - c + tq, rb_ref, h)
            bias_ref[2, h] = _shifted_bias(r - c, rb_ref, h)

        for h in range(N_HEADS):
            def sq_norm_max(c, mx):
                x = kt_ref[h * HEAD_DIM:(h + 1) * HEAD_DIM, pl.ds(pl.multiple_of(c * kb, kb), kb)].astype(F32)
                return jnp.maximum(mx, jnp.sum(x * x, axis=0, keepdims=True))
            mx = lax.fori_loop(0, kt_ref.shape[1] // kb, sq_norm_max, jnp.zeros((1, kb), F32))
            kn_ref[h] = jnp.broadcast_to(jnp.sqrt(jnp.max(mx, axis=-1, keepdims=True)), (1, LANES))

    for h in range(N_HEADS):
        keep = lo_half if h % 2 == 0 else jnp.logical_not(lo_half)
        pr = h // 2
        qm_ref[h] = jnp.where(keep, q_ref[:, pr * LANES:(pr + 1) * LANES].astype(F32), 0.0).astype(BF16)
        qim_ref[h] = jnp.where(keep, qi_ref[:, pr * LANES:(pr + 1) * LANES].astype(F32), 0.0).astype(BF16)

    wcols = [ws_ref[:, h:h + 1] for h in range(N_IDX_HEADS)]

    def score_chunk(c, carry):
        rmin, rmax = carry
        off = pl.multiple_of(c * kb, kb)
        kt = kit_ref[:, pl.ds(off, kb)]
        acc = jnp.zeros((tq, kb), F32)
        for h in range(N_IDX_HEADS):
            s = jnp.dot(qim_ref[h], kt, preferred_element_type=F32)
            acc = acc + jnp.maximum(s, 0.0) * wcols[h]
        key_pos = off + lax.broadcasted_iota(jnp.int32, (tq, kb), 1)
        valid = key_pos <= q_pos
        sc_ref[:, pl.ds(off, kb)] = jnp.where(valid, acc, NEG_INF)
        rmax = jnp.maximum(rmax, _lane_tile_max(jnp.where(valid, acc, NEG_INF)))
        rmin = jnp.minimum(rmin, _lane_tile_min(jnp.where(valid, acc, -NEG_INF)))
        return rmin, rmax

    rmin, rmax = lax.fori_loop(0, nch, score_chunk,
                               (jnp.full((tq, LANES), -NEG_INF, F32), jnp.full((tq, LANES), NEG_INF, F32)))
    rmin = jnp.min(rmin, axis=-1, keepdims=True)
    rmax = jnp.max(rmax, axis=-1, keepdims=True)

    topk = min(TOPK_MAX, kit_ref.shape[1] // 4)
    k_row = jnp.minimum(q_pos + 1, topk).astype(F32)
    nscan = ((i + 1) * tq + scan_kb - 1) // scan_kb
    thr = _topk_threshold_two_level(sc_ref, cand_ref, res_ref, nscan, scan_kb, k_row, (q_pos + 1).astype(F32),
                                    rmin, rmax, N_COARSE, n_bisect, n_index_bits)

    thr_row = jnp.concatenate([jnp.broadcast_to(thr, (tq, LANES))] * (scan_kb // LANES), axis=1)

    def to_mask(c, _):
        off = pl.multiple_of(c * scan_kb, scan_kb)
        sc_ref[:, pl.ds(off, scan_kb)] = jnp.where(sc_ref[:, pl.ds(off, scan_kb)] >= thr_row, 0.0, NEG_INF)
        return 0
    lax.fori_loop(0, nscan, to_mask, 0)

    def logits(c, h, off, with_bias):
        pr = h // 2
        kp = kt_ref[pr * LANES:(pr + 1) * LANES, pl.ds(off, kb)]
        s = jnp.dot(qm_ref[h], kp, preferred_element_type=F32) + sc_ref[:, pl.ds(off, kb)]
        if with_bias:
            tiles = [bias_ref[jnp.clip(c * sub + t - i + 2, 0, 2), h] for t in range(sub)]
            s = s + jnp.concatenate(tiles, axis=1)
        return s

    def max_chunk(c, with_bias):
        off = pl.multiple_of(c * kb, kb)
        for h in range(N_HEADS):
            m_ref[h] = jnp.maximum(m_ref[h], _lane_tile_max(logits(c, h, off, with_bias)))

    def acc_chunk(c, with_bias):
        off = pl.multiple_of(c * kb, kb)
        for pr in range(n_pairs):
            vp = vt_ref[pr * LANES:(pr + 1) * LANES, pl.ds(off, kb)]
            for hh in range(2):
                h = 2 * pr + hh
                p = jnp.exp2(logits(c, h, off, with_bias) - jnp.concatenate([m_ref[h]] * sub, axis=1))
                l_ref[h] = l_ref[h] + _lane_tile_sum(p)
                pv = lax.dot_general(p.astype(BF16), vp, _NT, preferred_element_type=F32)
                mine = lo_half if hh == 0 else jnp.logical_not(lo_half)
                acc_ref[pr] = acc_ref[pr] + jnp.where(mine, pv, 0.0)

    def sweep(fn):
        n_far = jnp.maximum((i * tq - MAX_DISTANCE + 1) // kb, 0)

        def far_body(c, _):
            fn(c, False)
            return 0

        def near_body(c, _):
            fn(c, True)
            return 0

        lax.fori_loop(0, n_far, far_body, 0)
        lax.fori_loop(n_far, nch, near_body, 0)

    def accumulate():
        l_ref[...] = jnp.zeros(l_ref.shape, F32)
        acc_ref[...] = jnp.zeros(acc_ref.shape, F32)
        sweep(acc_chunk)

    for h in range(N_HEADS):
        bias_max = jnp.float32(0.0)
        for bkt in range(NUM_BUCKETS - 1):
            bias_max = jnp.maximum(bias_max, rb_ref[bkt, h] - rb_ref[NUM_BUCKETS - 1, h])
        qf = qm_ref[h].astype(F32)
        q_norm = jnp.sqrt(jnp.sum(qf * qf, axis=-1, keepdims=True))
        m_ref[h] = jnp.broadcast_to(q_norm * kn_ref[h][:, :1] + bias_max * LOG2E, (tq, LANES))
    accumulate()
    l_min = jnp.min(jnp.sum(l_ref[0], axis=-1, keepdims=True))
    for h in range(1, N_HEADS):
        l_min = jnp.minimum(l_min, jnp.min(jnp.sum(l_ref[h], axis=-1, keepdims=True)))

    @pl.when(jnp.logical_not(l_min >= SHIFT_UNDERFLOW))
    def _():
        m_ref[...] = jnp.full(m_ref.shape, NEG_INF, F32)
        sweep(max_chunk)
        for h in range(N_HEADS):
            m_ref[h] = jnp.broadcast_to(jnp.max(m_ref[h], axis=-1, keepdims=True), (tq, LANES))
        accumulate()

    for pr in range(n_pairs):
        l0 = jnp.sum(l_ref[2 * pr], axis=-1, keepdims=True)
        l1 = jnp.sum(l_ref[2 * pr + 1], axis=-1, keepdims=True)
        o_ref[:, pr * LANES:(pr + 1) * LANES] = acc_ref[pr] / jnp.where(lo_half, l0, l1)


def _pattn_call(rel_bias, q, qi, ws, kit, kt16, vt16, tq, kb, scan_kb):
    bsz, s, d_attn = q.shape
    assert tq == LANES and MAX_DISTANCE <= tq and s % kb == 0 and kb % scan_kb == 0 and scan_kb % tq == 0
    n_index_bits = int(math.ceil(math.log2(s))) + 1
    blk = lambda c: pl.BlockSpec((None, tq, c), lambda b, i: (b, i, 0))
    whole = lambda r, c: pl.BlockSpec((None, r, c), lambda b, i: (b, 0, 0), pipeline_mode=pl.Buffered(1))
    return pl.pallas_call(
        functools.partial(_pattn_kernel, tq=tq, kb=kb, scan_kb=scan_kb, n_bisect=N_BISECT,
                          n_index_bits=n_index_bits),
        grid=(bsz, s // tq),
        in_specs=[pl.BlockSpec(memory_space=pltpu.SMEM),
                  blk(d_attn), blk(N_IDX_HEADS * IDX_DIM), blk(LANES),
                  whole(2 * IDX_DIM, s), whole(d_attn, s), whole(d_attn, s)],
        out_specs=blk(d_attn),
        out_shape=jax.ShapeDtypeStruct((bsz, s, d_attn), F32),
        scratch_shapes=[
            pltpu.VMEM((tq, s), F32),
            pltpu.VMEM((tq, 3 * LANES), F32),
            pltpu.VMEM((tq, 2 * LANES), F32),
            pltpu.VMEM((N_HEADS, tq, LANES), BF16),
            pltpu.VMEM((N_IDX_HEADS, tq, LANES), BF16),
            pltpu.VMEM((3, N_HEADS, tq, tq), F32),
            pltpu.VMEM((N_HEADS, 1, LANES), F32),
            pltpu.VMEM((N_HEADS, tq, LANES), F32),
            pltpu.VMEM((N_HEADS, tq, LANES), F32),
            pltpu.VMEM((N_HEADS // 2, tq, LANES), F32),
        ],
        compiler_params=_cparams(2),
        name="pattn",
    )(rel_bias, q, qi, ws, kit, kt16, vt16)


def _sattn_kernel(pt_ref, rb_ref, q_ref, qis_ref, w_ref, kin_ref, kn_ref, vn_ref,
                  cki_hbm, ck_hbm, cv_hbm, o_ref,
                  sc_ref, ibuf, kbuf, vbuf, isem, ksem, vsem, *, cp, page, n_bisect, n_index_bits):
    b = pl.program_id(0)
    n_pages = pt_ref.shape[1]
    nc = n_pages // cp
    ck = cp * page
    past = n_pages * page
    sn = q_ref.shape[0]
    rows = N_HEADS * sn
    d_attn = q_ref.shape[1]

    n_b = pl.num_programs(0)
    i_slots, kv_slots = ibuf.shape[0], kbuf.shape[0]

    def page_copies(cache, buf, sem, seq):
        row, c, slot = b + seq // nc, seq % nc, seq % buf.shape[0]
        return [pltpu.make_async_copy(cache.at[pt_ref[row, c * cp + j]],
                                      buf.at[slot, j], sem.at[slot])
                for j in range(cp)]

    def start(streams, seq):
        def go():
            for cache, buf, sem in streams:
                for cpy in page_copies(cache, buf, sem, seq):
                    cpy.start()
        if seq < nc:
            go()
        else:
            pl.when(b + 1 < n_b)(go)

    def wait(streams, seq):
        for cache, buf, sem in streams:
            for cpy in page_copies(cache, buf, sem, seq):
                cpy.wait()

    def chunk_of(buf, slot):
        return jnp.concatenate([buf[slot, j].astype(BF16) for j in range(cp)], axis=1)

    idx_stream = [(cki_hbm, ibuf, isem)]
    kv_streams = [(ck_hbm, kbuf, ksem), (cv_hbm, vbuf, vsem)]
    i_ahead, kv_ahead = i_slots - 1, kv_slots - 1

    @pl.when(b == 0)
    def _():
        for seq in range(i_ahead):
            start(idx_stream, seq)
        for seq in range(kv_ahead):
            start(kv_streams, seq)

    q_pos = past + lax.broadcasted_iota(jnp.int32, (sn, 1), 0)
    qis = qis_ref[...]
    w = w_ref[...]

    def head_sum(s):
        acc = jnp.zeros((sn, s.shape[1]), F32)
        for h in range(N_IDX_HEADS):
            acc = acc + jnp.maximum(s[h * sn:(h + 1) * sn], 0.0) * w[h * sn:(h + 1) * sn]
        return acc

    rmin = jnp.full((sn, LANES), -NEG_INF, F32)
    rmax = jnp.full((sn, LANES), NEG_INF, F32)
    for c in range(nc):
        slot = c % i_slots
        start(idx_stream, c + i_ahead)
        wait(idx_stream, c)
        acc = head_sum(jnp.dot(qis, chunk_of(ibuf, slot), preferred_element_type=F32))
        sc_ref[:, c * ck:(c + 1) * ck] = acc
        rmax = jnp.maximum(rmax, _lane_tile_max(acc))
        rmin = jnp.minimum(rmin, _lane_tile_min(acc))
    acc = head_sum(jnp.dot(qis, kin_ref[...], preferred_element_type=F32))
    lane = lax.broadcasted_iota(jnp.int32, (sn, LANES), 1)
    valid = jnp.logical_and(lane < sn, past + lane <= q_pos)
    sc_ref[:, past:past + LANES] = jnp.where(valid, acc, NEG_INF)
    sc_ref[:, past + LANES:past + ck] = jnp.full((sn, ck - LANES), NEG_INF, F32)
    rmax = jnp.max(jnp.maximum(rmax, jnp.where(valid, acc, NEG_INF)), axis=-1, keepdims=True)
    rmin = jnp.min(jnp.minimum(rmin, jnp.where(valid, acc, -NEG_INF)), axis=-1, keepdims=True)

    topk = min(TOPK_MAX, (past + sn) // 4)
    k_row = jnp.minimum(q_pos + 1, topk).astype(F32)
    thr = _topk_threshold(sc_ref, nc + 1, ck, k_row, rmin, rmax, n_bisect, n_index_bits)

    r_head = lax.broadcasted_iota(jnp.int32, (rows, d_attn), 0) // sn
    c_head = lax.broadcasted_iota(jnp.int32, (rows, d_attn), 1) // HEAD_DIM
    own = r_head == c_head
    qbd = jnp.where(own, jnp.tile(q_ref[...].astype(F32), (N_HEADS, 1)), 0.0).astype(BF16)

    def softmax_step(state, s, vals_t):
        m_prev, l_prev, acc_prev = state
        m_next = jnp.maximum(m_prev, jnp.max(s, axis=-1, keepdims=True))
        alpha = jnp.exp2(m_prev - m_next)
        p = jnp.exp2(s - m_next[:, :1])
        l_next = alpha * l_prev + _lane_tile_sum(p)
        pv = lax.dot_general(p.astype(BF16), vals_t, _NT, preferred_element_type=F32)
        return m_next, l_next, acc_prev * alpha[:, :1] + pv

    def biased(s, key0):
        dist = q_pos - (key0 + lax.broadcasted_iota(jnp.int32, (sn, s.shape[1]), 1))
        return s + jnp.concatenate([_shifted_bias(dist, rb_ref, h) for h in range(N_HEADS)], axis=0)

    state = (jnp.full((rows, LANES), NEG_INF, F32), jnp.zeros((rows, LANES), F32),
             jnp.zeros((rows, d_attn), F32))
    for c in range(nc):
        slot = c % kv_slots
        start(kv_streams, c + kv_ahead)
        wait(kv_streams, c)
        madd = jnp.where(sc_ref[:, c * ck:(c + 1) * ck] >= thr, 0.0, NEG_INF)
        s = jnp.dot(qbd, chunk_of(kbuf, slot), preferred_element_type=F32)
        s = s + jnp.tile(madd, (N_HEADS, 1))
        if (c + 1) * ck + MAX_DISTANCE > past:
            s = biased(s, c * ck)
        state = softmax_step(state, s, chunk_of(vbuf, slot))
    madd = jnp.where(sc_ref[:, past:past + LANES] >= thr, 0.0, NEG_INF)
    s = jnp.dot(qbd, kn_ref[...], preferred_element_type=F32)
    s = biased(s + jnp.tile(madd, (N_HEADS, 1)), past)
    _, l_fin, acc = softmax_step(state, s, vn_ref[...])
    res = jnp.where(own, acc / jnp.sum(l_fin, axis=-1, keepdims=True), 0.0)
    out = res[0:sn]
    for h in range(1, N_HEADS):
        out = out + res[h * sn:(h + 1) * sn]
    o_ref[...] = out


def _sattn_call(page_table, rel_bias, q, qis, w, kin_t, kn_t, vn_t, cache_kidx_t, cache_k_t, cache_v_t, cp):
    db, sn, d_attn = q.shape
    n_pages = page_table.shape[1]
    page = cache_k_t.shape[2]
    assert n_pages % cp == 0 and sn == 8 and page == LANES and cp >= 2 and MAX_DISTANCE <= cp * page
    ck = cp * page
    rows = N_HEADS * sn
    nc = n_pages // cp
    kv_slots = math.gcd(nc, KV_SLOTS)
    assert nc % 2 == 0 and kv_slots >= 2
    n_index_bits = int(math.ceil(math.log2(n_pages * page + ck))) + 1
    per_b = lambda r, c: pl.BlockSpec((None, r, c), lambda b, pt: (b, 0, 0))
    grid_spec = pltpu.PrefetchScalarGridSpec(
        num_scalar_prefetch=1,
        grid=(db,),
        in_specs=[pl.BlockSpec(memory_space=pltpu.SMEM),
                  per_b(sn, d_attn), per_b(rows, IDX_DIM), per_b(rows, 1),
                  per_b(IDX_DIM, LANES), per_b(d_attn, LANES), per_b(d_attn, LANES),
                  pl.BlockSpec(memory_space=pl.ANY), pl.BlockSpec(memory_space=pl.ANY),
                  pl.BlockSpec(memory_space=pl.ANY)],
        out_specs=per_b(sn, d_attn),
        scratch_shapes=[
            pltpu.VMEM((sn, n_pages * page + ck), F32),
            pltpu.VMEM((2, cp, IDX_DIM, page), F32),
            pltpu.VMEM((kv_slots, cp, d_attn, page), F32),
            pltpu.VMEM((kv_slots, cp, d_attn, page), F32),
            pltpu.SemaphoreType.DMA((2,)), pltpu.SemaphoreType.DMA((kv_slots,)), pltpu.SemaphoreType.DMA((kv_slots,)),
        ],
    )
    return pl.pallas_call(
        functools.partial(_sattn_kernel, cp=cp, page=page, n_bisect=N_BISECT, n_index_bits=n_index_bits),
        grid_spec=grid_spec,
        out_shape=jax.ShapeDtypeStruct((db, sn, d_attn), F32),
        compiler_params=_cparams(1),
        name="sattn",
    )(page_table, rel_bias, q, qis, w, kin_t, kn_t, vn_t, cache_kidx_t, cache_k_t, cache_v_t)


def _gelu_tanh(x):
    return 0.5 * x * (1.0 + jnp.tanh(math.sqrt(2.0 / math.pi) * (x + 0.044715 * x * x * x)))


def _ssm_kernel(u_ref, h0r_ref, h0i_ref, lr_ref, li_ref, ldt_ref, bre_ref, bim_ref, cre_ref, cim_ref,
                d_ref, wglu_ref, o_ref, hr_out_ref, hi_out_ref,
                lbr_ref, lbi_ref, bbr_ref, bbi_ref, bur_ref, bui_ref, hre_ref, him_ref, sr_ref, si_ref,
                *, nb, unroll):
    step = pl.program_id(0)
    tt = u_ref.shape[0]
    d_ssm = u_ref.shape[1]

    @pl.when(step == 0)
    def _():
        lr, li = lr_ref[...], li_ref[...]
        dt = jnp.exp(ldt_ref[...])
        mag = jnp.exp(lr * dt)
        lbr, lbi = mag * jnp.cos(li * dt), mag * jnp.sin(li * dt)
        den = lr * lr + li * li
        xr, xi = lbr - 1.0, lbi
        cr = (xr * lr + xi * li) / den
        ci = (xi * lr - xr * li) / den
        lbr_ref[...] = lbr
        lbi_ref[...] = lbi
        bbr_ref[...] = (cr * bre_ref[...] - ci * bim_ref[...]).astype(BF16)
        bbi_ref[...] = (cr * bim_ref[...] + ci * bre_ref[...]).astype(BF16)
        sr_ref[...] = h0r_ref[...]
        si_ref[...] = h0i_ref[...]

    u = u_ref[...]
    u16 = u.astype(BF16)
    bur_ref[...] = jnp.dot(u16, bbr_ref[...], preferred_element_type=F32)
    bui_ref[...] = jnp.dot(u16, bbi_ref[...], preferred_element_type=F32)
    lbr = jnp.broadcast_to(lbr_ref[...], (nb, lbr_ref.shape[1]))
    lbi = jnp.broadcast_to(lbi_ref[...], (nb, lbi_ref.shape[1]))

    def scan(t, carry):
        hr, hi = carry
        r0 = pl.multiple_of(t * nb, nb)
        nr = lbr * hr - lbi * hi + bur_ref[pl.ds(r0, nb), :]
        ni = lbr * hi + lbi * hr + bui_ref[pl.ds(r0, nb), :]
        hre_ref[pl.ds(r0, nb), :] = nr
        him_ref[pl.ds(r0, nb), :] = ni
        return nr, ni

    hr, hi = lax.fori_loop(0, tt // nb, scan, (sr_ref[...], si_ref[...]), unroll=unroll)
    sr_ref[...] = hr
    si_ref[...] = hi
    hr_out_ref[...] = hr
    hi_out_ref[...] = hi

    y = (jnp.dot(hre_ref[...].astype(BF16), cre_ref[...], preferred_element_type=F32)
         - jnp.dot(him_ref[...].astype(BF16), cim_ref[...], preferred_element_type=F32)
         + d_ref[...] * u)
    z = _gelu_tanh(y)
    g = jnp.dot(z.astype(BF16), wglu_ref[...], preferred_element_type=F32)
    o_ref[...] = g[:, :d_ssm] / (1.0 + jnp.exp(-g[:, d_ssm:]))


def _ssm_call(u, h0r, h0i, lr, li, ldt, bre, bim, cre, cim, d, wglu, nb, tt):
    n, d_ssm = u.shape
    ns = lr.shape[1]
    assert n % tt == 0 and tt % nb == 0
    unroll = 8 if (tt // nb) % 8 == 0 else 1
    state = jax.ShapeDtypeStruct((nb, ns), F32)
    return pl.pallas_call(
        functools.partial(_ssm_kernel, nb=nb, unroll=unroll),
        grid=(n // tt,),
        in_specs=[pl.BlockSpec((tt, d_ssm), lambda i: (i, 0)),
                  _resident((nb, ns)), _resident((nb, ns)),
                  _resident((1, ns)), _resident((1, ns)), _resident((1, ns)),
                  _resident((d_ssm, ns)), _resident((d_ssm, ns)),
                  _resident((ns, d_ssm)), _resident((ns, d_ssm)),
                  _resident((1, d_ssm)), _resident((d_ssm, 2 * d_ssm))],
        out_specs=(pl.BlockSpec((tt, d_ssm), lambda i: (i, 0)),
                   pl.BlockSpec((nb, ns), lambda i: (0, 0)), pl.BlockSpec((nb, ns), lambda i: (0, 0))),
        out_shape=(jax.ShapeDtypeStruct((n, d_ssm), F32), state, state),
        scratch_shapes=[
            pltpu.VMEM((1, ns), F32), pltpu.VMEM((1, ns), F32),
            pltpu.VMEM((d_ssm, ns), BF16), pltpu.VMEM((d_ssm, ns), BF16),
            pltpu.VMEM((tt, ns), F32), pltpu.VMEM((tt, ns), F32),
            pltpu.VMEM((tt, ns), F32), pltpu.VMEM((tt, ns), F32),
            pltpu.VMEM((nb, ns), F32), pltpu.VMEM((nb, ns), F32),
        ],
        compiler_params=_cparams(1),
        name="ssm",
    )(u, h0r, h0i, lr, li, ldt, bre, bim, cre, cim, d, wglu)


def _finish_kernel(x_ref, attn_ref, ssm_ref, gi_ref, bi_ref, wo_ref, g1_ref, b1_ref, wr_ref, br_ref,
                   wg_ref, wu_ref, wd_ref, g2_ref, b2_ref, o_ref, *, alpha, pre_ln):
    x = x_ref[...]
    if pre_ln:
        x = _layer_norm(x, gi_ref[...], bi_ref[...])
    d_attn = attn_ref.shape[1]
    mix = (jnp.dot(attn_ref[...].astype(BF16), wo_ref[:d_attn, :], preferred_element_type=F32)
           + jnp.dot(ssm_ref[...].astype(BF16), wo_ref[d_attn:, :], preferred_element_type=F32))
    x1 = _layer_norm(alpha * x + mix, g1_ref[...], b1_ref[...])
    x16 = x1.astype(BF16)

    lg = jnp.dot(x16, wr_ref[...], preferred_element_type=F32) + br_ref[...]
    lane = lax.broadcasted_iota(jnp.int32, lg.shape, 1)
    big = jnp.int32(1 << 20)
    is_grp = lane < N_EXPERT_GROUPS
    gl = jnp.where(is_grp, lg, NEG_INF)
    gmax = jnp.max(gl, axis=-1, keepdims=True)
    g_sel = jnp.min(jnp.where(gl == gmax, lane, big), axis=-1, keepdims=True)
    p_grp = 1.0 / jnp.sum(jnp.where(is_grp, jnp.exp(gl - gmax), 0.0), axis=-1, keepdims=True)
    e_idx = lane - N_EXPERT_GROUPS
    in_sel = jnp.logical_and(jnp.logical_and(e_idx >= 0, e_idx < N_EXPERTS),
                             e_idx // EXPERTS_PER_GROUP == g_sel)
    el = jnp.where(in_sel, lg, NEG_INF)
    v1 = jnp.max(el, axis=-1, keepdims=True)
    i1 = jnp.min(jnp.where(el == v1, lane, big), axis=-1, keepdims=True)
    el2 = jnp.where(lane == i1, NEG_INF, el)
    v2 = jnp.max(el2, axis=-1, keepdims=True)
    i2 = jnp.min(jnp.where(el2 == v2, lane, big), axis=-1, keepdims=True)
    e21 = jnp.exp(v2 - v1)
    w1 = p_grp / (1.0 + e21)
    w2 = p_grp * e21 / (1.0 + e21)
    gates = jnp.where(lane == i1, w1, jnp.where(lane == i2, w2, 0.0))

    y = jnp.zeros(x1.shape, F32)
    for e in range(N_EXPERTS):
        gate = gates[:, N_EXPERT_GROUPS + e:N_EXPERT_GROUPS + e + 1]
        hg = jnp.dot(x16, wg_ref[e], preferred_element_type=F32)
        hu = jnp.dot(x16, wu_ref[e], preferred_element_type=F32)
        hid = hg / (1.0 + jnp.exp(-hg)) * hu * gate
        y = y + jnp.dot(hid.astype(BF16), wd_ref[e], preferred_element_type=F32)
    o_ref[...] = _layer_norm(alpha * x1 + y, g2_ref[...], b2_ref[...])


def _finish_call(x, attn, ssm, gi, bi, wo, g1, b1, wr, br, wg, wu, wd, g2, b2, alpha, pre_ln, tm):
    n, d = x.shape
    row = lambda c: pl.BlockSpec((tm, c), lambda i: (i, 0))
    return pl.pallas_call(
        functools.partial(_finish_kernel, alpha=alpha, pre_ln=pre_ln),
        grid=(n // tm,),
        in_specs=[row(d), row(attn.shape[1]), row(ssm.shape[1]),
                  _resident(gi.shape), _resident(bi.shape), _resident(wo.shape),
                  _resident(g1.shape), _resident(b1.shape), _resident(wr.shape), _resident(br.shape),
                  _resident(wg.shape), _resident(wu.shape), _resident(wd.shape),
                  _resident(g2.shape), _resident(b2.shape)],
        out_specs=row(d),
        out_shape=jax.ShapeDtypeStruct((n, d), F32),
        compiler_params=_cparams(1),
        name="finish",
    )(x, attn, ssm, gi, bi, wo, g1, b1, wr, br, wg, wu, wd, g2, b2)


def _row(a):
    return a.reshape(1, -1).astype(F32)


def _block_diag(a):
    g, r, c = a.shape
    return jnp.einsum("grc,gh->grhc", a, jnp.eye(g, dtype=a.dtype)).reshape(g * r, g * c)


def _pad_lanes(a):
    return jnp.pad(a, ((0, 0), (0, 0), (0, LANES - a.shape[2])))


def _pick_tile(n, pref):
    t = min(n, pref)
    while n % t:
        t //= 2
    return t


def kernel(x_prompt, x_sample, cache_k, cache_v, cache_kidx, state_ssm_re, state_ssm_im, page_table, ln_in_g, ln_in_b, rel_bias, w_in, idx_ln_g, idx_ln_b, ssm_lam_re, ssm_lam_im, ssm_log_dt, ssm_b_re, ssm_b_im, ssm_c_re, ssm_c_im, ssm_d, w_glu, w_out, ln_mix_g, ln_mix_b, w_router_grp, b_router_grp, w_router_exp, b_router_exp, w_gate, w_up, w_down, ln_ffn_g, ln_ffn_b):
    bsz, seq, d_model = x_prompt.shape
    dbsz, dseq, _ = x_sample.shape
    depth = w_in.shape[0]
    d_attn = N_HEADS * HEAD_DIM
    n_grp = ssm_lam_re.shape[1]
    d_ssm = n_grp * SSM_GROUP
    n_state = n_grp * SSM_STATE
    n_pool, page = cache_k.shape[1], cache_k.shape[2]
    alpha = (2.0 * depth) ** 0.25
    sizes = (d_attn, d_attn, d_attn, N_IDX_HEADS * IDX_DIM, IDX_DIM, N_IDX_HEADS, d_ssm)
    cuts = [0] + [int(c) for c in np.cumsum(sizes)]
    rel_bias = rel_bias.astype(F32)
    page_table = page_table.astype(jnp.int32)

    xp = x_prompt.reshape(bsz * seq, d_model)
    xs = x_sample.reshape(dbsz * dseq, d_model)
    outs = [[] for _ in range(10)]
    for l in range(depth):
        pre_ln = l == 0
        seg = [w_in[l][:, cuts[j]:cuts[j + 1]] for j in range(7)]
        pad = jnp.zeros((d_model, LANES - N_IDX_HEADS), w_in.dtype)
        w_row = jnp.concatenate([seg[0], seg[3], seg[6], seg[5], pad], axis=1).astype(BF16)
        w_t = jnp.concatenate([seg[1], seg[2], seg[4]], axis=1).T.astype(BF16)
        proj_w = (_row(ln_in_g), _row(ln_in_b), w_row, w_t,
                  idx_ln_g[l].reshape(IDX_DIM, 1).astype(F32), idx_ln_b[l].reshape(IDX_DIM, 1).astype(F32))
        ssm_w = (_row(ssm_lam_re[l]), _row(ssm_lam_im[l]), _row(jnp.repeat(ssm_log_dt[l], SSM_STATE)),
                 _block_diag(jnp.swapaxes(ssm_b_re[l], 1, 2)).astype(F32),
                 _block_diag(jnp.swapaxes(ssm_b_im[l], 1, 2)).astype(F32),
                 _block_diag(jnp.swapaxes(ssm_c_re[l], 1, 2)).astype(BF16),
                 _block_diag(jnp.swapaxes(ssm_c_im[l], 1, 2)).astype(BF16),
                 _row(ssm_d[l]), w_glu[l].astype(BF16))
        wr = jnp.concatenate([w_router_grp[l], jnp.transpose(w_router_exp[l], (1, 0, 2)).reshape(d_model, N_EXPERTS)], axis=1)
        wr = jnp.pad(wr, ((0, 0), (0, LANES - wr.shape[1]))).astype(BF16)
        br = jnp.pad(jnp.concatenate([b_router_grp[l], b_router_exp[l].reshape(-1)]),
                     (0, LANES - N_EXPERT_GROUPS - N_EXPERTS)).reshape(1, LANES).astype(F32)
        fin_w = (_row(ln_in_g), _row(ln_in_b), w_out[l].astype(BF16), _row(ln_mix_g[l]), _row(ln_mix_b[l]), wr, br,
                 w_gate[l].astype(BF16), w_up[l].astype(BF16), w_down[l].astype(BF16),
                 _row(ln_ffn_g[l]), _row(ln_ffn_b[l]))

        attn_l, k_l, v_l, ki_l, u_l = [], [], [], [], []
        for b in range(bsz):
            q, qi, u, ws, kt, vt, kt16, vt16, kit, kit16 = _proj_call(
                xp[b * seq:(b + 1) * seq], *proj_w, d_attn, d_ssm, pre_ln, _pick_tile(seq, PROJ_ROWS))
            attn_l.append(_pattn_call(rel_bias, q[None], qi[None], ws[None], kit16[None], kt16[None], vt16[None],
                                      tq=QUERY_BLOCK, kb=KEY_CHUNK, scan_kb=SCAN_CHUNK)[0])
            k_l.append(jnp.transpose(kt.reshape(N_HEADS, HEAD_DIM, seq), (2, 0, 1)))
            v_l.append(jnp.transpose(vt.reshape(N_HEADS, HEAD_DIM, seq), (2, 0, 1)))
            ki_l.append(kit.T)
            u_l.append(u)
        u_tm = jnp.stack(u_l, axis=1).reshape(seq * bsz, d_ssm)
        h0 = jnp.zeros((bsz, n_state), F32)
        ssm_tm, hr, hi = _ssm_call(u_tm, h0, h0, *ssm_w, nb=bsz, tt=_pick_tile(seq, SSM_ROWS) * bsz)
        ssm_o = jnp.swapaxes(ssm_tm.reshape(seq, bsz, d_ssm), 0, 1).reshape(bsz * seq, d_ssm)
        xp = _finish_call(xp, jnp.concatenate(attn_l, axis=0), ssm_o, *fin_w, alpha=alpha, pre_ln=pre_ln,
                          tm=_pick_tile(bsz * seq, FINISH_ROWS))
        outs[0].append(jnp.stack(k_l))
        outs[1].append(jnp.stack(v_l))
        outs[2].append(jnp.stack(ki_l))
        outs[3].append(hr.reshape(bsz, n_grp, SSM_STATE).astype(state_ssm_re.dtype))
        outs[4].append(hi.reshape(bsz, n_grp, SSM_STATE).astype(state_ssm_im.dtype))

        n_s = dbsz * dseq
        q, qi, u, ws, kt, vt, kt16, vt16, kit, kit16 = _proj_call(xs, *proj_w, d_attn, d_ssm, pre_ln,
                                                                  _pick_tile(n_s, PROJ_ROWS))
        rows = N_IDX_HEADS * dseq
        qis = jnp.swapaxes(qi.reshape(dbsz, dseq, N_IDX_HEADS, IDX_DIM), 1, 2).reshape(dbsz, rows, IDX_DIM)
        wis = jnp.swapaxes(ws[:, :N_IDX_HEADS].reshape(dbsz, dseq, N_IDX_HEADS), 1, 2).reshape(dbsz, rows, 1)
        per_seq_t = lambda a: _pad_lanes(jnp.swapaxes(a.reshape(a.shape[0], dbsz, dseq), 0, 1))
        ck_t = jnp.transpose(cache_k[l], (0, 2, 3, 1)).reshape(n_pool, d_attn, page)
        cv_t = jnp.transpose(cache_v[l], (0, 2, 3, 1)).reshape(n_pool, d_attn, page)
        cki_t = jnp.swapaxes(cache_kidx[l], 1, 2)
        attn = _sattn_call(page_table, rel_bias, q.reshape(dbsz, dseq, d_attn), qis, wis,
                           per_seq_t(kit16[:IDX_DIM]), per_seq_t(kt16), per_seq_t(vt16), cki_t, ck_t, cv_t,
                           cp=min(PAGES_PER_CHUNK, page_table.shape[1]))
        u_tm = jnp.swapaxes(u.reshape(dbsz, dseq, d_ssm), 0, 1).reshape(n_s, d_ssm)
        ssm_tm, hr, hi = _ssm_call(u_tm, state_ssm_re[l].reshape(dbsz, n_state).astype(F32),
                                   state_ssm_im[l].reshape(dbsz, n_state).astype(F32), *ssm_w, nb=dbsz, tt=n_s)
        ssm_o = jnp.swapaxes(ssm_tm.reshape(dseq, dbsz, d_ssm), 0, 1).reshape(n_s, d_ssm)
        xs = _finish_call(xs, attn.reshape(n_s, d_attn), ssm_o, *fin_w, alpha=alpha, pre_ln=pre_ln,
                          tm=_pick_tile(n_s, FINISH_ROWS))
        outs[5].append(kt.T.reshape(dbsz, dseq, N_HEADS, HEAD_DIM))
        outs[6].append(vt.T.reshape(dbsz, dseq, N_HEADS, HEAD_DIM))
        outs[7].append(kit.T.reshape(dbsz, dseq, IDX_DIM))
        outs[8].append(hr.reshape(dbsz, n_grp, SSM_STATE).astype(state_ssm_re.dtype))
        outs[9].append(hi.reshape(dbsz, n_grp, SSM_STATE).astype(state_ssm_im.dtype))

    return (xp.reshape(bsz, seq, d_model), xs.reshape(dbsz, dseq, d_model)) + tuple(jnp.stack(o) for o in outs)
```

```python
import functools
import math

import numpy as np
import jax
import jax.numpy as jnp
from jax import lax
from jax.experimental import pallas as pl
from jax.experimental.pallas import tpu as pltpu

N_HEADS = 8
HEAD_DIM = 64
N_IDX_HEADS = 8
IDX_DIM = 64
TOPK_MAX = 256
NUM_BUCKETS = 32
MAX_DISTANCE = 128
SSM_GROUP = 16
SSM_STATE = 64
N_EXPERT_GROUPS = 4
EXPERTS_PER_GROUP = 4
N_EXPERTS = N_EXPERT_GROUPS * EXPERTS_PER_GROUP
LN_EPS = 1e-5
NEG_INF = -1e30
LOG2E = math.log2(math.e)

LANES = 128
VMEM_LIMIT = 56 * 1024 * 1024

PROJ_ROWS = 512
FINISH_ROWS = 512
SSM_ROWS = 512
QUERY_BLOCK = LANES
KEY_CHUNK = 1024
SCAN_CHUNK = 512
PAGES_PER_CHUNK = 16
KV_SLOTS = 4
N_BISECT = 16
N_COARSE = 8
N_CAND = 4
N_FINE = 8
SHIFT_UNDERFLOW = 2.0 ** -100

F32 = jnp.float32
BF16 = jnp.bfloat16
_NT = (((1,), (1,)), ((), ()))


def _cparams(n_grid, vmem=VMEM_LIMIT):
    return pltpu.CompilerParams(dimension_semantics=("arbitrary",) * n_grid, vmem_limit_bytes=vmem)


def _resident(shape):
    nd = len(shape)
    return pl.BlockSpec(shape, lambda *_: (0,) * nd, pipeline_mode=pl.Buffered(1))


def _layer_norm(x, g, b):
    mu = jnp.mean(x, axis=-1, keepdims=True)
    xc = x - mu
    var = jnp.mean(xc * xc, axis=-1, keepdims=True)
    return xc * lax.rsqrt(var + LN_EPS) * g + b


def _proj_kernel(x_ref, g_ref, b_ref, wr_ref, wt_ref, ig_ref, ib_ref,
                 q_ref, qi_ref, u_ref, ws_ref, kt_ref, vt_ref, ktb_ref, vtb_ref, kit_ref, kitb_ref,
                 *, d_attn, d_ssm, pre_ln):
    xn = x_ref[...]
    if pre_ln:
        xn = _layer_norm(xn, g_ref[...], b_ref[...])
    x16 = xn.astype(BF16)
    d_qi = N_IDX_HEADS * IDX_DIM
    hr = jnp.dot(x16, wr_ref[...], preferred_element_type=F32)
    ht = lax.dot_general(wt_ref[...], x16, _NT, preferred_element_type=F32)
    q_ref[...] = (hr[:, :d_attn] * (HEAD_DIM ** -0.5 * LOG2E)).astype(BF16)
    qi_ref[...] = (hr[:, d_attn:d_attn + d_qi] * IDX_DIM ** -0.5).astype(BF16)
    u_ref[...] = hr[:, d_attn + d_qi:d_attn + d_qi + d_ssm]
    ws_ref[...] = hr[:, d_attn + d_qi + d_ssm:] * N_IDX_HEADS ** -0.5
    kt = ht[:d_attn]
    vt = ht[d_attn:2 * d_attn]
    kt_ref[...] = kt
    vt_ref[...] = vt
    ktb_ref[...] = kt.astype(BF16)
    vtb_ref[...] = vt.astype(BF16)
    ki = ht[2 * d_attn:]
    mu = jnp.mean(ki, axis=0, keepdims=True)
    kc = ki - mu
    var = jnp.mean(kc * kc, axis=0, keepdims=True)
    kin = kc * lax.rsqrt(var + LN_EPS) * ig_ref[...] + ib_ref[...]
    kit_ref[...] = kin
    kin16 = kin.astype(BF16)
    kitb_ref[...] = jnp.concatenate([kin16, kin16], axis=0)


def _proj_call(x, ln_g, ln_b, w_row, w_t, ig, ib, d_attn, d_ssm, pre_ln, tm):
    n, d = x.shape
    d_qi = N_IDX_HEADS * IDX_DIM
    row = lambda c, dt: (jax.ShapeDtypeStruct((n, c), dt), pl.BlockSpec((tm, c), lambda i: (i, 0)))
    col = lambda r, dt: (jax.ShapeDtypeStruct((r, n), dt), pl.BlockSpec((r, tm), lambda i: (0, i)))
    outs = [row(d_attn, BF16), row(d_qi, BF16), row(d_ssm, F32), row(LANES, F32),
            col(d_attn, F32), col(d_attn, F32), col(d_attn, BF16), col(d_attn, BF16),
            col(IDX_DIM, F32), col(2 * IDX_DIM, BF16)]
    return pl.pallas_call(
        functools.partial(_proj_kernel, d_attn=d_attn, d_ssm=d_ssm, pre_ln=pre_ln),
        grid=(n // tm,),
        in_specs=[pl.BlockSpec((tm, d), lambda i: (i, 0)), _resident((1, d)), _resident((1, d)),
                  _resident(w_row.shape), _resident(w_t.shape), _resident((IDX_DIM, 1)), _resident((IDX_DIM, 1))],
        out_specs=tuple(o[1] for o in outs),
        out_shape=tuple(o[0] for o in outs),
        compiler_params=_cparams(1),
        name="proj",
    )(x, ln_g, ln_b, w_row, w_t, ig, ib)


def _rel_bucket(dist):
    n = jnp.maximum(dist, 0)
    max_exact = NUM_BUCKETS // 2
    scaled = jnp.log(jnp.maximum(n, 1).astype(F32) / max_exact) / math.log(MAX_DISTANCE / max_exact)
    large = jnp.minimum(max_exact + jnp.floor(scaled * (NUM_BUCKETS - max_exact)).astype(jnp.int32),
                        NUM_BUCKETS - 1)
    return jnp.where(n < max_exact, n, large)


def _shifted_bias(dist, rb_ref, h):
    bkt = _rel_bucket(dist)
    far = rb_ref[NUM_BUCKETS - 1, h]
    out = jnp.zeros(dist.shape, F32)
    for b in range(NUM_BUCKETS - 1):
        out = jnp.where(bkt == b, rb_ref[b, h] - far, out)
    return out * LOG2E


def _lane_tile_sum(x):
    acc = x[:, :LANES]
    for t in range(1, x.shape[1] // LANES):
        acc = acc + x[:, t * LANES:(t + 1) * LANES]
    return acc


def _lane_tile_max(x):
    acc = x[:, :LANES]
    for t in range(1, x.shape[1] // LANES):
        acc = jnp.maximum(acc, x[:, t * LANES:(t + 1) * LANES])
    return acc


def _lane_tile_min(x):
    acc = x[:, :LANES]
    for t in range(1, x.shape[1] // LANES):
        acc = jnp.minimum(acc, x[:, t * LANES:(t + 1) * LANES])
    return acc


def _chunk_of(ref, kb):
    return lambda c: ref[:, pl.ds(pl.multiple_of(c * kb, kb), kb)]


def _count_ge(ref, nch, kb, t):
    chunk = _chunk_of(ref, kb)

    def body(c, cnt):
        return cnt + _lane_tile_sum(jnp.where(chunk(c) >= t, 1.0, 0.0))
    cnt = lax.fori_loop(0, nch, body, jnp.zeros((ref.shape[0], LANES), F32))
    return jnp.sum(cnt, axis=-1, keepdims=True)


def _kth_largest(ref, nch, kb, k_row, rmin, rmax, n_bisect):
    rows = ref.shape[0]
    chunk = _chunk_of(ref, kb)

    def bisect(_, lohi):
        lo, hi = lohi
        mid = 0.5 * (lo + hi)
        ok = _count_ge(ref, nch, kb, mid) >= k_row
        return jnp.where(ok, mid, lo), jnp.where(ok, hi, mid)

    _, hi = lax.fori_loop(0, n_bisect, bisect, (rmin, rmax))

    def count_ge_and_next(m):
        def body(c, carry):
            cnt, nxt = carry
            x = chunk(c)
            cnt = cnt + _lane_tile_sum(jnp.where(x >= m, 1.0, 0.0))
            nxt = jnp.maximum(nxt, _lane_tile_max(jnp.where(x < m, x, NEG_INF)))
            return cnt, nxt
        cnt, nxt = lax.fori_loop(0, nch, body, (jnp.zeros((rows, LANES), F32),
                                                jnp.full((rows, LANES), NEG_INF, F32)))
        return jnp.sum(cnt, axis=-1, keepdims=True), jnp.max(nxt, axis=-1, keepdims=True)

    def first_le(t):
        def body(c, nxt):
            x = chunk(c)
            return jnp.maximum(nxt, _lane_tile_max(jnp.where(x <= t, x, NEG_INF)))
        nxt = lax.fori_loop(0, nch, body, jnp.full((rows, LANES), NEG_INF, F32))
        return jnp.max(nxt, axis=-1, keepdims=True)

    def walk_cond(st):
        return st[3] > 0.0

    def walk_body(st):
        m, cfin, done, _ = st
        cnt, nxt = count_ge_and_next(m)
        ok = jnp.logical_and(cnt >= k_row, done < 0.5)
        cfin = jnp.where(ok, cnt, cfin)
        done = jnp.where(cnt >= k_row, 1.0, done)
        m = jnp.where(done > 0.5, m, nxt)
        return m, cfin, done, jnp.sum(1.0 - done)

    zeros = jnp.zeros((rows, 1), F32)
    thr, cfin, _, _ = lax.while_loop(walk_cond, walk_body, (first_le(hi), zeros, zeros, jnp.float32(rows)))
    return thr, cfin


def _drop_ties(ref, nch, kb, thr, k_row, n_index_bits):
    rows = ref.shape[0]
    chunk = _chunk_of(ref, kb)

    def count_gt():
        def body(c, cnt):
            return cnt + _lane_tile_sum(jnp.where(chunk(c) > thr, 1.0, 0.0))
        cnt = lax.fori_loop(0, nch, body, jnp.zeros((rows, LANES), F32))
        return jnp.sum(cnt, axis=-1, keepdims=True)

    need = k_row - count_gt()

    def count_eq_upto(j):
        def body(c, cnt):
            x = chunk(c)
            idx = (c * kb + lax.broadcasted_iota(jnp.int32, x.shape, 1)).astype(F32)
            hit = jnp.where(x == thr, jnp.where(idx <= j, 1.0, 0.0), 0.0)
            return cnt + _lane_tile_sum(hit)
        cnt = lax.fori_loop(0, nch, body, jnp.zeros((rows, LANES), F32))
        return jnp.sum(cnt, axis=-1, keepdims=True)

    def ibisect(_, lohi):
        lo, hi_i = lohi
        mid = jnp.floor(0.5 * (lo + hi_i))
        ok = count_eq_upto(mid) >= need
        return jnp.where(ok, lo, mid), jnp.where(ok, mid, hi_i)

    width = jnp.float32(1.0) * (nch * kb)
    _, cut = lax.fori_loop(0, n_index_bits, ibisect,
                           (jnp.full((rows, 1), -1.0, F32), jnp.zeros((rows, 1), F32) + (width - 1.0)))

    def drop(c, _):
        off = pl.multiple_of(c * kb, kb)
        x = ref[:, pl.ds(off, kb)]
        idx = (c * kb + lax.broadcasted_iota(jnp.int32, x.shape, 1)).astype(F32)
        dropped = jnp.where(x == thr, jnp.where(idx > cut, 1.0, 0.0), 0.0)
        ref[:, pl.ds(off, kb)] = jnp.where(dropped > 0.5, NEG_INF, x)
        return 0
    lax.fori_loop(0, nch, drop, 0)


def _topk_threshold(sc_ref, nch, kb, k_row, rmin, rmax, n_bisect, n_index_bits):
    thr, cfin = _kth_largest(sc_ref, nch, kb, k_row, rmin, rmax, n_bisect)

    @pl.when(jnp.sum(jnp.where(cfin > k_row, 1.0, 0.0)) > 0.0)
    def _():
        _drop_ties(sc_ref, nch, kb, thr, k_row, n_index_bits)

    return thr


def _topk_threshold_two_level(sc_ref, cand_ref, res_ref, nch, kb, k_row, n_valid, rmin, rmax,
                              n_coarse, n_bisect, n_index_bits):
    rows = sc_ref.shape[0]
    chunk = _chunk_of(sc_ref, kb)
    hi0 = rmax + jnp.maximum(jnp.abs(rmax), 1e-30) * 2.0 ** -20

    def bisect(_, st):
        lo, hi, c_lo, c_hi = st
        mid = 0.5 * (lo + hi)
        c = _count_ge(sc_ref, nch, kb, mid)
        ok = c >= k_row
        return (jnp.where(ok, mid, lo), jnp.where(ok, hi, mid), jnp.where(ok, c, c_lo), jnp.where(ok, c_hi, c))

    lo, hi, c_lo, c_hi = lax.fori_loop(0, n_coarse, bisect, (rmin, hi0, n_valid, jnp.zeros((rows, 1), F32)))

    n_cand = cand_ref.shape[1] // LANES

    def capture(c, top):
        x = chunk(c)
        x = jnp.where(x < hi, x, NEG_INF)
        top = list(top)
        for t in range(kb // LANES):
            v = x[:, t * LANES:(t + 1) * LANES]
            for r in range(n_cand - 1):
                top[r], v = jnp.maximum(top[r], v), jnp.minimum(top[r], v)
            top[-1] = jnp.maximum(top[-1], v)
        return tuple(top)

    neg = jnp.full((rows, LANES), NEG_INF, F32)
    top = lax.fori_loop(0, nch, capture, (neg,) * n_cand)
    for r in range(n_cand):
        cand_ref[:, r * LANES:(r + 1) * LANES] = top[r]
    kept = _count_ge(cand_ref, 1, n_cand * LANES, lo)
    missed = jnp.sum(jnp.where(kept == c_lo - c_hi, 0.0, 1.0))

    @pl.when(missed == 0.0)
    def _():
        thr, cnt = _kth_largest(cand_ref, 1, n_cand * LANES, k_row - c_hi, lo, hi, N_FINE)
        res_ref[:, 0:LANES] = jnp.broadcast_to(thr, (rows, LANES))
        res_ref[:, LANES:2 * LANES] = jnp.broadcast_to(cnt + c_hi, (rows, LANES))

    @pl.when(missed > 0.0)
    def _():
        thr, cnt = _kth_largest(sc_ref, nch, kb, k_row, rmin, rmax, n_bisect)
        res_ref[:, 0:LANES] = jnp.broadcast_to(thr, (rows, LANES))
        res_ref[:, LANES:2 * LANES] = jnp.broadcast_to(cnt, (rows, LANES))

    thr = res_ref[:, 0:1]
    cfin = res_ref[:, LANES:LANES + 1]

    @pl.when(jnp.sum(jnp.where(cfin > k_row, 1.0, 0.0)) > 0.0)
    def _():
        _drop_ties(sc_ref, nch, kb, thr, k_row, n_index_bits)

    return thr


def _pattn_kernel(rb_ref, q_ref, qi_ref, ws_ref, kit_ref, kt_ref, vt_ref, o_ref,
                  sc_ref, cand_ref, res_ref, qm_ref, qim_ref, bias_ref, kn_ref, m_ref, l_ref, acc_ref,
                  *, tq, kb, scan_kb, n_bisect, n_index_bits):
    i = pl.program_id(1)
    n_pairs = N_HEADS // 2
    nch = ((i + 1) * tq + kb - 1) // kb
    sub = kb // tq
    lane = lax.broadcasted_iota(jnp.int32, (tq, LANES), 1)
    lo_half = lane < HEAD_DIM
    q_pos = i * tq + lax.broadcasted_iota(jnp.int32, (tq, 1), 0)

    @pl.when(i == 0)
    def _():
        r = lax.broadcasted_iota(jnp.int32, (tq, tq), 0)
        c = lax.broadcasted_iota(jnp.int32, (tq, tq), 1)
        for h in range(N_HEADS):
            bias_ref[0, h] = jnp.zeros((tq, tq), F32)
            bias_ref[1, h] = _shifted_bias(r - c + tq, rb_ref, h)
            bias_ref[2, h] = _shifted_bias(r - c, rb_ref, h)

        for h in range(N_HEADS):
            def sq_norm_max(c, mx):
                x = kt_ref[h * HEAD_DIM:(h + 1) * HEAD_DIM, pl.ds(pl.multiple_of(c * kb, kb), kb)].astype(F32)
                return jnp.maximum(mx, jnp.sum(x * x, axis=0, keepdims=True))
            mx = lax.fori_loop(0, kt_ref.shape[1] // kb, sq_norm_max, jnp.zeros((1, kb), F32))
            kn_ref[h] = jnp.broadcast_to(jnp.sqrt(jnp.max(mx, axis=-1, keepdims=True)), (1, LANES))

    for h in range(N_HEADS):
        keep = lo_half if h % 2 == 0 else jnp.logical_not(lo_half)
        pr = h // 2
        qm_ref[h] = jnp.where(keep, q_ref[:, pr * LANES:(pr + 1) * LANES].astype(F32), 0.0).astype(BF16)
        qim_ref[h] = jnp.where(keep, qi_ref[:, pr * LANES:(pr + 1) * LANES].astype(F32), 0.0).astype(BF16)

    wcols = [ws_ref[:, h:h + 1] for h in range(N_IDX_HEADS)]

    def score_chunk(c, carry):
        rmin, rmax = carry
        off = pl.multiple_of(c * kb, kb)
        kt = kit_ref[:, pl.ds(off, kb)]
        acc = jnp.zeros((tq, kb), F32)
        for h in range(N_IDX_HEADS):
            s = jnp.dot(qim_ref[h], kt, preferred_element_type=F32)
            acc = acc + jnp.maximum(s, 0.0) * wcols[h]
        key_pos = off + lax.broadcasted_iota(jnp.int32, (tq, kb), 1)
        valid = key_pos <= q_pos
        sc_ref[:, pl.ds(off, kb)] = jnp.where(valid, acc, NEG_INF)
        rmax = jnp.maximum(rmax, _lane_tile_max(jnp.where(valid, acc, NEG_INF)))
        rmin = jnp.minimum(rmin, _lane_tile_min(jnp.where(valid, acc, -NEG_INF)))
        return rmin, rmax

    rmin, rmax = lax.fori_loop(0, nch, score_chunk,
                               (jnp.full((tq, LANES), -NEG_INF, F32), jnp.full((tq, LANES), NEG_INF, F32)))
    rmin = jnp.min(rmin, axis=-1, keepdims=True)
    rmax = jnp.max(rmax, axis=-1, keepdims=True)

    topk = min(TOPK_MAX, kit_ref.shape[1] // 4)
    k_row = jnp.minimum(q_pos + 1, topk).astype(F32)
    nscan = ((i + 1) * tq + scan_kb - 1) // scan_kb
    thr = _topk_threshold_two_level(sc_ref, cand_ref, res_ref, nscan, scan_kb, k_row, (q_pos + 1).astype(F32),
                                    rmin, rmax, N_COARSE, n_bisect, n_index_bits)

    thr_row = jnp.concatenate([jnp.broadcast_to(thr, (tq, LANES))] * (scan_kb // LANES), axis=1)

    def to_mask(c, _):
        off = pl.multiple_of(c * scan_kb, scan_kb)
        sc_ref[:, pl.ds(off, scan_kb)] = jnp.where(sc_ref[:, pl.ds(off, scan_kb)] >= thr_row, 0.0, NEG_INF)
        return 0
    lax.fori_loop(0, nscan, to_mask, 0)

    def logits(c, h, off, with_bias):
        pr = h // 2
        kp = kt_ref[pr * LANES:(pr + 1) * LANES, pl.ds(off, kb)]
        s = jnp.dot(qm_ref[h], kp, preferred_element_type=F32) + sc_ref[:, pl.ds(off, kb)]
        if with_bias:
            tiles = [bias_ref[jnp.clip(c * sub + t - i + 2, 0, 2), h] for t in range(sub)]
            s = s + jnp.concatenate(tiles, axis=1)
        return s

    def max_chunk(c, with_bias):
        off = pl.multiple_of(c * kb, kb)
        for h in range(N_HEADS):
            m_ref[h] = jnp.maximum(m_ref[h], _lane_tile_max(logits(c, h, off, with_bias)))

    def acc_chunk(c, with_bias):
        off = pl.multiple_of(c * kb, kb)
        for pr in range(n_pairs):
            vp = vt_ref[pr * LANES:(pr + 1) * LANES, pl.ds(off, kb)]
            for hh in range(2):
                h = 2 * pr + hh
                p = jnp.exp2(logits(c, h, off, with_bias) - jnp.concatenate([m_ref[h]] * sub, axis=1))
                l_ref[h] = l_ref[h] + _lane_tile_sum(p)
                pv = lax.dot_general(p.astype(BF16), vp, _NT, preferred_element_type=F32)
                mine = lo_half if hh == 0 else jnp.logical_not(lo_half)
                acc_ref[pr] = acc_ref[pr] + jnp.where(mine, pv, 0.0)

    def sweep(fn):
        n_far = jnp.maximum((i * tq - MAX_DISTANCE + 1) // kb, 0)

        def far_body(c, _):
            fn(c, False)
            return 0

        def near_body(c, _):
            fn(c, True)
            return 0

        lax.fori_loop(0, n_far, far_body, 0)
        lax.fori_loop(n_far, nch, near_body, 0)

    def accumulate():
        l_ref[...] = jnp.zeros(l_ref.shape, F32)
        acc_ref[...] = jnp.zeros(acc_ref.shape, F32)
        sweep(acc_chunk)

    for h in range(N_HEADS):
        bias_max = jnp.float32(0.0)
        for bkt in range(NUM_BUCKETS - 1):
            bias_max = jnp.maximum(bias_max, rb_ref[bkt, h] - rb_ref[NUM_BUCKETS - 1, h])
        qf = qm_ref[h].astype(F32)
        q_norm = jnp.sqrt(jnp.sum(qf * qf, axis=-1, keepdims=True))
        m_ref[h] = jnp.broadcast_to(q_norm * kn_ref[h][:, :1] + bias_max * LOG2E, (tq, LANES))
    accumulate()
    l_min = jnp.min(jnp.sum(l_ref[0], axis=-1, keepdims=True))
    for h in range(1, N_HEADS):
        l_min = jnp.minimum(l_min, jnp.min(jnp.sum(l_ref[h], axis=-1, keepdims=True)))

    @pl.when(jnp.logical_not(l_min >= SHIFT_UNDERFLOW))
    def _():
        m_ref[...] = jnp.full(m_ref.shape, NEG_INF, F32)
        sweep(max_chunk)
        for h in range(N_HEADS):
            m_ref[h] = jnp.broadcast_to(jnp.max(m_ref[h], axis=-1, keepdims=True), (tq, LANES))
        accumulate()

    for pr in range(n_pairs):
        l0 = jnp.sum(l_ref[2 * pr], axis=-1, keepdims=True)
        l1 = jnp.sum(l_ref[2 * pr + 1], axis=-1, keepdims=True)
        o_ref[:, pr * LANES:(pr + 1) * LANES] = acc_ref[pr] / jnp.where(lo_half, l0, l1)


def _pattn_call(rel_bias, q, qi, ws, kit, kt16, vt16, tq, kb, scan_kb):
    bsz, s, d_attn = q.shape
    assert tq == LANES and MAX_DISTANCE <= tq and s % kb == 0 and kb % scan_kb == 0 and scan_kb % tq == 0
    n_index_bits = int(math.ceil(math.log2(s))) + 1
    blk = lambda c: pl.BlockSpec((None, tq, c), lambda b, i: (b, i, 0))
    whole = lambda r, c: pl.BlockSpec((None, r, c), lambda b, i: (b, 0, 0), pipeline_mode=pl.Buffered(1))
    return pl.pallas_call(
        functools.partial(_pattn_kernel, tq=tq, kb=kb, scan_kb=scan_kb, n_bisect=N_BISECT,
                          n_index_bits=n_index_bits),
        grid=(bsz, s // tq),
        in_specs=[pl.BlockSpec(memory_space=pltpu.SMEM),
                  blk(d_attn), blk(N_IDX_HEADS * IDX_DIM), blk(LANES),
                  whole(2 * IDX_DIM, s), whole(d_attn, s), whole(d_attn, s)],
        out_specs=blk(d_attn),
        out_shape=jax.ShapeDtypeStruct((bsz, s, d_attn), F32),
        scratch_shapes=[
            pltpu.VMEM((tq, s), F32),
            pltpu.VMEM((tq, N_CAND * LANES), F32),
            pltpu.VMEM((tq, 2 * LANES), F32),
            pltpu.VMEM((N_HEADS, tq, LANES), BF16),
            pltpu.VMEM((N_IDX_HEADS, tq, LANES), BF16),
            pltpu.VMEM((3, N_HEADS, tq, tq), F32),
            pltpu.VMEM((N_HEADS, 1, LANES), F32),
            pltpu.VMEM((N_HEADS, tq, LANES), F32),
            pltpu.VMEM((N_HEADS, tq, LANES), F32),
            pltpu.VMEM((N_HEADS // 2, tq, LANES), F32),
        ],
        compiler_params=_cparams(2),
        name="pattn",
    )(rel_bias, q, qi, ws, kit, kt16, vt16)


def _sattn_kernel(pt_ref, rb_ref, q_ref, qis_ref, w_ref, kin_ref, kn_ref, vn_ref,
                  cki_hbm, ck_hbm, cv_hbm, o_ref,
                  sc_ref, ibuf, kbuf, vbuf, isem, ksem, vsem, *, cp, page, n_bisect, n_index_bits):
    b = pl.program_id(0)
    n_pages = pt_ref.shape[1]
    nc = n_pages // cp
    ck = cp * page
    past = n_pages * page
    sn = q_ref.shape[0]
    rows = N_HEADS * sn
    d_attn = q_ref.shape[1]

    n_b = pl.num_programs(0)
    i_slots, kv_slots = ibuf.shape[0], kbuf.shape[0]

    def page_copies(cache, buf, sem, seq):
        row, c, slot = b + seq // nc, seq % nc, seq % buf.shape[0]
        return [pltpu.make_async_copy(cache.at[pt_ref[row, c * cp + j]],
                                      buf.at[slot, j], sem.at[slot])
                for j in range(cp)]

    def start(streams, seq):
        def go():
            for cache, buf, sem in streams:
                for cpy in page_copies(cache, buf, sem, seq):
                    cpy.start()
        if seq < nc:
            go()
        else:
            pl.when(b + 1 < n_b)(go)

    def wait(streams, seq):
        for cache, buf, sem in streams:
            for cpy in page_copies(cache, buf, sem, seq):
                cpy.wait()

    def chunk_of(buf, slot):
        return jnp.concatenate([buf[slot, j].astype(BF16) for j in range(cp)], axis=1)

    idx_stream = [(cki_hbm, ibuf, isem)]
    kv_streams = [(ck_hbm, kbuf, ksem), (cv_hbm, vbuf, vsem)]
    i_ahead, kv_ahead = i_slots - 1, kv_slots - 1

    @pl.when(b == 0)
    def _():
        for seq in range(i_ahead):
            start(idx_stream, seq)
        for seq in range(kv_ahead):
            start(kv_streams, seq)

    q_pos = past + lax.broadcasted_iota(jnp.int32, (sn, 1), 0)
    qis = qis_ref[...]
    w = w_ref[...]

    def head_sum(s):
        acc = jnp.zeros((sn, s.shape[1]), F32)
        for h in range(N_IDX_HEADS):
            acc = acc + jnp.maximum(s[h * sn:(h + 1) * sn], 0.0) * w[h * sn:(h + 1) * sn]
        return acc

    rmin = jnp.full((sn, LANES), -NEG_INF, F32)
    rmax = jnp.full((sn, LANES), NEG_INF, F32)
    for c in range(nc):
        slot = c % i_slots
        start(idx_stream, c + i_ahead)
        wait(idx_stream, c)
        acc = head_sum(jnp.dot(qis, chunk_of(ibuf, slot), preferred_element_type=F32))
        sc_ref[:, c * ck:(c + 1) * ck] = acc
        rmax = jnp.maximum(rmax, _lane_tile_max(acc))
        rmin = jnp.minimum(rmin, _lane_tile_min(acc))
    acc = head_sum(jnp.dot(qis, kin_ref[...], preferred_element_type=F32))
    lane = lax.broadcasted_iota(jnp.int32, (sn, LANES), 1)
    valid = jnp.logical_and(lane < sn, past + lane <= q_pos)
    sc_ref[:, past:past + LANES] = jnp.where(valid, acc, NEG_INF)
    sc_ref[:, past + LANES:past + ck] = jnp.full((sn, ck - LANES), NEG_INF, F32)
    rmax = jnp.max(jnp.maximum(rmax, jnp.where(valid, acc, NEG_INF)), axis=-1, keepdims=True)
    rmin = jnp.min(jnp.minimum(rmin, jnp.where(valid, acc, -NEG_INF)), axis=-1, keepdims=True)

    topk = min(TOPK_MAX, (past + sn) // 4)
    k_row = jnp.minimum(q_pos + 1, topk).astype(F32)
    thr = _topk_threshold(sc_ref, nc + 1, ck, k_row, rmin, rmax, n_bisect, n_index_bits)

    r_head = lax.broadcasted_iota(jnp.int32, (rows, d_attn), 0) // sn
    c_head = lax.broadcasted_iota(jnp.int32, (rows, d_attn), 1) // HEAD_DIM
    own = r_head == c_head
    qbd = jnp.where(own, jnp.tile(q_ref[...].astype(F32), (N_HEADS, 1)), 0.0).astype(BF16)

    def softmax_step(state, s, vals_t):
        m_prev, l_prev, acc_prev = state
        m_next = jnp.maximum(m_prev, jnp.max(s, axis=-1, keepdims=True))
        alpha = jnp.exp2(m_prev - m_next)
        p = jnp.exp2(s - m_next[:, :1])
        l_next = alpha * l_prev + _lane_tile_sum(p)
        pv = lax.dot_general(p.astype(BF16), vals_t, _NT, preferred_element_type=F32)
        return m_next, l_next, acc_prev * alpha[:, :1] + pv

    def biased(s, key0):
        dist = q_pos - (key0 + lax.broadcasted_iota(jnp.int32, (sn, s.shape[1]), 1))
        return s + jnp.concatenate([_shifted_bias(dist, rb_ref, h) for h in range(N_HEADS)], axis=0)

    state = (jnp.full((rows, LANES), NEG_INF, F32), jnp.zeros((rows, LANES), F32),
             jnp.zeros((rows, d_attn), F32))
    for c in range(nc):
        slot = c % kv_slots
        start(kv_streams, c + kv_ahead)
        wait(kv_streams, c)
        madd = jnp.where(sc_ref[:, c * ck:(c + 1) * ck] >= thr, 0.0, NEG_INF)
        s = jnp.dot(qbd, chunk_of(kbuf, slot), preferred_element_type=F32)
        s = s + jnp.tile(madd, (N_HEADS, 1))
        if (c + 1) * ck + MAX_DISTANCE > past:
            s = biased(s, c * ck)
        state = softmax_step(state, s, chunk_of(vbuf, slot))
    madd = jnp.where(sc_ref[:, past:past + LANES] >= thr, 0.0, NEG_INF)
    s = jnp.dot(qbd, kn_ref[...], preferred_element_type=F32)
    s = biased(s + jnp.tile(madd, (N_HEADS, 1)), past)
    _, l_fin, acc = softmax_step(state, s, vn_ref[...])
    res = jnp.where(own, acc / jnp.sum(l_fin, axis=-1, keepdims=True), 0.0)
    out = res[0:sn]
    for h in range(1, N_HEADS):
        out = out + res[h * sn:(h + 1) * sn]
    o_ref[...] = out


def _sattn_call(page_table, rel_bias, q, qis, w, kin_t, kn_t, vn_t, cache_kidx_t, cache_k_t, cache_v_t, cp):
    db, sn, d_attn = q.shape
    n_pages = page_table.shape[1]
    page = cache_k_t.shape[2]
    assert n_pages % cp == 0 and sn == 8 and page == LANES and cp >= 2 and MAX_DISTANCE <= cp * page
    ck = cp * page
    rows = N_HEADS * sn
    nc = n_pages // cp
    kv_slots = math.gcd(nc, KV_SLOTS)
    assert nc % 2 == 0 and kv_slots >= 2
    n_index_bits = int(math.ceil(math.log2(n_pages * page + ck))) + 1
    per_b = lambda r, c: pl.BlockSpec((None, r, c), lambda b, pt: (b, 0, 0))
    grid_spec = pltpu.PrefetchScalarGridSpec(
        num_scalar_prefetch=1,
        grid=(db,),
        in_specs=[pl.BlockSpec(memory_space=pltpu.SMEM),
                  per_b(sn, d_attn), per_b(rows, IDX_DIM), per_b(rows, 1),
                  per_b(IDX_DIM, LANES), per_b(d_attn, LANES), per_b(d_attn, LANES),
                  pl.BlockSpec(memory_space=pl.ANY), pl.BlockSpec(memory_space=pl.ANY),
                  pl.BlockSpec(memory_space=pl.ANY)],
        out_specs=per_b(sn, d_attn),
        scratch_shapes=[
            pltpu.VMEM((sn, n_pages * page + ck), F32),
            pltpu.VMEM((2, cp, IDX_DIM, page), F32),
            pltpu.VMEM((kv_slots, cp, d_attn, page), F32),
            pltpu.VMEM((kv_slots, cp, d_attn, page), F32),
            pltpu.SemaphoreType.DMA((2,)), pltpu.SemaphoreType.DMA((kv_slots,)), pltpu.SemaphoreType.DMA((kv_slots,)),
        ],
    )
    return pl.pallas_call(
        functools.partial(_sattn_kernel, cp=cp, page=page, n_bisect=N_BISECT, n_index_bits=n_index_bits),
        grid_spec=grid_spec,
        out_shape=jax.ShapeDtypeStruct((db, sn, d_attn), F32),
        compiler_params=_cparams(1),
        name="sattn",
    )(page_table, rel_bias, q, qis, w, kin_t, kn_t, vn_t, cache_kidx_t, cache_k_t, cache_v_t)


def _gelu_tanh(x):
    return 0.5 * x * (1.0 + jnp.tanh(math.sqrt(2.0 / math.pi) * (x + 0.044715 * x * x * x)))


def _ssm_kernel(u_ref, h0r_ref, h0i_ref, lr_ref, li_ref, ldt_ref, bre_ref, bim_ref, cre_ref, cim_ref,
                d_ref, wglu_ref, o_ref, hr_out_ref, hi_out_ref,
                lbr_ref, lbi_ref, bbr_ref, bbi_ref, bur_ref, bui_ref, hre_ref, him_ref, sr_ref, si_ref,
                *, nb, unroll):
    step = pl.program_id(0)
    tt = u_ref.shape[0]
    d_ssm = u_ref.shape[1]

    @pl.when(step == 0)
    def _():
        lr, li = lr_ref[...], li_ref[...]
        dt = jnp.exp(ldt_ref[...])
        mag = jnp.exp(lr * dt)
        lbr, lbi = mag * jnp.cos(li * dt), mag * jnp.sin(li * dt)
        den = lr * lr + li * li
        xr, xi = lbr - 1.0, lbi
        cr = (xr * lr + xi * li) / den
        ci = (xi * lr - xr * li) / den
        lbr_ref[...] = lbr
        lbi_ref[...] = lbi
        bbr_ref[...] = (cr * bre_ref[...] - ci * bim_ref[...]).astype(BF16)
        bbi_ref[...] = (cr * bim_ref[...] + ci * bre_ref[...]).astype(BF16)
        sr_ref[...] = h0r_ref[...]
        si_ref[...] = h0i_ref[...]

    u = u_ref[...]
    u16 = u.astype(BF16)
    n_tiles = d_ssm // LANES
    st = lbr_ref.shape[1] // n_tiles
    for j in range(n_tiles):
        uj = u16[:, j * LANES:(j + 1) * LANES]
        bur_ref[:, j * st:(j + 1) * st] = jnp.dot(uj, bbr_ref[j * LANES:(j + 1) * LANES, j * st:(j + 1) * st],
                                                  preferred_element_type=F32)
        bui_ref[:, j * st:(j + 1) * st] = jnp.dot(uj, bbi_ref[j * LANES:(j + 1) * LANES, j * st:(j + 1) * st],
                                                  preferred_element_type=F32)
    lbr = jnp.broadcast_to(lbr_ref[...], (nb, lbr_ref.shape[1]))
    lbi = jnp.broadcast_to(lbi_ref[...], (nb, lbi_ref.shape[1]))

    def scan(t, carry):
        hr, hi = carry
        r0 = pl.multiple_of(t * nb, nb)
        nr = lbr * hr - lbi * hi + bur_ref[pl.ds(r0, nb), :]
        ni = lbr * hi + lbi * hr + bui_ref[pl.ds(r0, nb), :]
        hre_ref[pl.ds(r0, nb), :] = nr
        him_ref[pl.ds(r0, nb), :] = ni
        return nr, ni

    hr, hi = lax.fori_loop(0, tt // nb, scan, (sr_ref[...], si_ref[...]), unroll=unroll)
    sr_ref[...] = hr
    si_ref[...] = hi
    hr_out_ref[...] = hr
    hi_out_ref[...] = hi

    y_tiles = []
    for j in range(n_tiles):
        rows, cols = slice(j * st, (j + 1) * st), slice(j * LANES, (j + 1) * LANES)
        y_tiles.append(jnp.dot(hre_ref[:, rows].astype(BF16), cre_ref[rows, cols], preferred_element_type=F32)
                       - jnp.dot(him_ref[:, rows].astype(BF16), cim_ref[rows, cols], preferred_element_type=F32))
    y = jnp.concatenate(y_tiles, axis=1) + d_ref[...] * u
    z = _gelu_tanh(y)
    g = jnp.dot(z.astype(BF16), wglu_ref[...], preferred_element_type=F32)
    o_ref[...] = g[:, :d_ssm] / (1.0 + jnp.exp(-g[:, d_ssm:]))


def _ssm_call(u, h0r, h0i, lr, li, ldt, bre, bim, cre, cim, d, wglu, nb, tt):
    n, d_ssm = u.shape
    ns = lr.shape[1]
    assert n % tt == 0 and tt % nb == 0 and d_ssm % LANES == 0 and LANES % SSM_GROUP == 0
    unroll = 8 if (tt // nb) % 8 == 0 else 1
    state = jax.ShapeDtypeStruct((nb, ns), F32)
    return pl.pallas_call(
        functools.partial(_ssm_kernel, nb=nb, unroll=unroll),
        grid=(n // tt,),
        in_specs=[pl.BlockSpec((tt, d_ssm), lambda i: (i, 0)),
                  _resident((nb, ns)), _resident((nb, ns)),
                  _resident((1, ns)), _resident((1, ns)), _resident((1, ns)),
                  _resident((d_ssm, ns)), _resident((d_ssm, ns)),
                  _resident((ns, d_ssm)), _resident((ns, d_ssm)),
                  _resident((1, d_ssm)), _resident((d_ssm, 2 * d_ssm))],
        out_specs=(pl.BlockSpec((tt, d_ssm), lambda i: (i, 0)),
                   pl.BlockSpec((nb, ns), lambda i: (0, 0)), pl.BlockSpec((nb, ns), lambda i: (0, 0))),
        out_shape=(jax.ShapeDtypeStruct((n, d_ssm), F32), state, state),
        scratch_shapes=[
            pltpu.VMEM((1, ns), F32), pltpu.VMEM((1, ns), F32),
            pltpu.VMEM((d_ssm, ns), BF16), pltpu.VMEM((d_ssm, ns), BF16),
            pltpu.VMEM((tt, ns), F32), pltpu.VMEM((tt, ns), F32),
            pltpu.VMEM((tt, ns), F32), pltpu.VMEM((tt, ns), F32),
            pltpu.VMEM((nb, ns), F32), pltpu.VMEM((nb, ns), F32),
        ],
        compiler_params=_cparams(1),
        name="ssm",
    )(u, h0r, h0i, lr, li, ldt, bre, bim, cre, cim, d, wglu)


def _finish_kernel(x_ref, attn_ref, ssm_ref, gi_ref, bi_ref, wo_ref, g1_ref, b1_ref, wr_ref, br_ref,
                   wg_ref, wu_ref, wd_ref, g2_ref, b2_ref, o_ref, *, alpha, pre_ln):
    x = x_ref[...]
    if pre_ln:
        x = _layer_norm(x, gi_ref[...], bi_ref[...])
    d_attn = attn_ref.shape[1]
    mix = (jnp.dot(attn_ref[...].astype(BF16), wo_ref[:d_attn, :], preferred_element_type=F32)
           + jnp.dot(ssm_ref[...].astype(BF16), wo_ref[d_attn:, :], preferred_element_type=F32))
    x1 = _layer_norm(alpha * x + mix, g1_ref[...], b1_ref[...])
    x16 = x1.astype(BF16)

    lg = jnp.dot(x16, wr_ref[...], preferred_element_type=F32) + br_ref[...]
    lane = lax.broadcasted_iota(jnp.int32, lg.shape, 1)
    big = jnp.int32(1 << 20)
    is_grp = lane < N_EXPERT_GROUPS
    gl = jnp.where(is_grp, lg, NEG_INF)
    gmax = jnp.max(gl, axis=-1, keepdims=True)
    g_sel = jnp.min(jnp.where(gl == gmax, lane, big), axis=-1, keepdims=True)
    p_grp = 1.0 / jnp.sum(jnp.where(is_grp, jnp.exp(gl - gmax), 0.0), axis=-1, keepdims=True)
    e_idx = lane - N_EXPERT_GROUPS
    in_sel = jnp.logical_and(jnp.logical_and(e_idx >= 0, e_idx < N_EXPERTS),
                             e_idx // EXPERTS_PER_GROUP == g_sel)
    el = jnp.where(in_sel, lg, NEG_INF)
    v1 = jnp.max(el, axis=-1, keepdims=True)
    i1 = jnp.min(jnp.where(el == v1, lane, big), axis=-1, keepdims=True)
    el2 = jnp.where(lane == i1, NEG_INF, el)
    v2 = jnp.max(el2, axis=-1, keepdims=True)
    i2 = jnp.min(jnp.where(el2 == v2, lane, big), axis=-1, keepdims=True)
    e21 = jnp.exp(v2 - v1)
    w1 = p_grp / (1.0 + e21)
    w2 = p_grp * e21 / (1.0 + e21)
    gates = jnp.where(lane == i1, w1, jnp.where(lane == i2, w2, 0.0))

    y = jnp.zeros(x1.shape, F32)
    for e in range(N_EXPERTS):
        gate = gates[:, N_EXPERT_GROUPS + e:N_EXPERT_GROUPS + e + 1]
        hg = jnp.dot(x16, wg_ref[e], preferred_element_type=F32)
        hu = jnp.dot(x16, wu_ref[e], preferred_element_type=F32)
        hid = hg / (1.0 + jnp.exp(-hg)) * hu * gate
        y = y + jnp.dot(hid.astype(BF16), wd_ref[e], preferred_element_type=F32)
    o_ref[...] = _layer_norm(alpha * x1 + y, g2_ref[...], b2_ref[...])


def _finish_call(x, attn, ssm, gi, bi, wo, g1, b1, wr, br, wg, wu, wd, g2, b2, alpha, pre_ln, tm):
    n, d = x.shape
    row = lambda c: pl.BlockSpec((tm, c), lambda i: (i, 0))
    return pl.pallas_call(
        functools.partial(_finish_kernel, alpha=alpha, pre_ln=pre_ln),
        grid=(n // tm,),
        in_specs=[row(d), row(attn.shape[1]), row(ssm.shape[1]),
                  _resident(gi.shape), _resident(bi.shape), _resident(wo.shape),
                  _resident(g1.shape), _resident(b1.shape), _resident(wr.shape), _resident(br.shape),
                  _resident(wg.shape), _resident(wu.shape), _resident(wd.shape),
                  _resident(g2.shape), _resident(b2.shape)],
        out_specs=row(d),
        out_shape=jax.ShapeDtypeStruct((n, d), F32),
        compiler_params=_cparams(1),
        name="finish",
    )(x, attn, ssm, gi, bi, wo, g1, b1, wr, br, wg, wu, wd, g2, b2)


def _row(a):
    return a.reshape(1, -1).astype(F32)


def _block_diag(a):
    g, r, c = a.shape
    return jnp.einsum("grc,gh->grhc", a, jnp.eye(g, dtype=a.dtype)).reshape(g * r, g * c)


def _pad_lanes(a):
    return jnp.pad(a, ((0, 0), (0, 0), (0, LANES - a.shape[2])))


def _pick_tile(n, pref):
    t = min(n, pref)
    while n % t:
        t //= 2
    return t


def kernel(x_prompt, x_sample, cache_k, cache_v, cache_kidx, state_ssm_re, state_ssm_im, page_table, ln_in_g, ln_in_b, rel_bias, w_in, idx_ln_g, idx_ln_b, ssm_lam_re, ssm_lam_im, ssm_log_dt, ssm_b_re, ssm_b_im, ssm_c_re, ssm_c_im, ssm_d, w_glu, w_out, ln_mix_g, ln_mix_b, w_router_grp, b_router_grp, w_router_exp, b_router_exp, w_gate, w_up, w_down, ln_ffn_g, ln_ffn_b):
    bsz, seq, d_model = x_prompt.shape
    dbsz, dseq, _ = x_sample.shape
    depth = w_in.shape[0]
    d_attn = N_HEADS * HEAD_DIM
    n_grp = ssm_lam_re.shape[1]
    d_ssm = n_grp * SSM_GROUP
    n_state = n_grp * SSM_STATE
    n_pool, page = cache_k.shape[1], cache_k.shape[2]
    alpha = (2.0 * depth) ** 0.25
    sizes = (d_attn, d_attn, d_attn, N_IDX_HEADS * IDX_DIM, IDX_DIM, N_IDX_HEADS, d_ssm)
    cuts = [0] + [int(c) for c in np.cumsum(sizes)]
    rel_bias = rel_bias.astype(F32)
    page_table = page_table.astype(jnp.int32)

    xp = x_prompt.reshape(bsz * seq, d_model)
    xs = x_sample.reshape(dbsz * dseq, d_model)
    outs = [[] for _ in range(10)]
    for l in range(depth):
        pre_ln = l == 0
        seg = [w_in[l][:, cuts[j]:cuts[j + 1]] for j in range(7)]
        pad = jnp.zeros((d_model, LANES - N_IDX_HEADS), w_in.dtype)
        w_row = jnp.concatenate([seg[0], seg[3], seg[6], seg[5], pad], axis=1).astype(BF16)
        w_t = jnp.concatenate([seg[1], seg[2], seg[4]], axis=1).T.astype(BF16)
        proj_w = (_row(ln_in_g), _row(ln_in_b), w_row, w_t,
                  idx_ln_g[l].reshape(IDX_DIM, 1).astype(F32), idx_ln_b[l].reshape(IDX_DIM, 1).astype(F32))
        ssm_w = (_row(ssm_lam_re[l]), _row(ssm_lam_im[l]), _row(jnp.repeat(ssm_log_dt[l], SSM_STATE)),
                 _block_diag(jnp.swapaxes(ssm_b_re[l], 1, 2)).astype(F32),
                 _block_diag(jnp.swapaxes(ssm_b_im[l], 1, 2)).astype(F32),
                 _block_diag(jnp.swapaxes(ssm_c_re[l], 1, 2)).astype(BF16),
                 _block_diag(jnp.swapaxes(ssm_c_im[l], 1, 2)).astype(BF16),
                 _row(ssm_d[l]), w_glu[l].astype(BF16))
        wr = jnp.concatenate([w_router_grp[l], jnp.transpose(w_router_exp[l], (1, 0, 2)).reshape(d_model, N_EXPERTS)], axis=1)
        wr = jnp.pad(wr, ((0, 0), (0, LANES - wr.shape[1]))).astype(BF16)
        br = jnp.pad(jnp.concatenate([b_router_grp[l], b_router_exp[l].reshape(-1)]),
                     (0, LANES - N_EXPERT_GROUPS - N_EXPERTS)).reshape(1, LANES).astype(F32)
        fin_w = (_row(ln_in_g), _row(ln_in_b), w_out[l].astype(BF16), _row(ln_mix_g[l]), _row(ln_mix_b[l]), wr, br,
                 w_gate[l].astype(BF16), w_up[l].astype(BF16), w_down[l].astype(BF16),
                 _row(ln_ffn_g[l]), _row(ln_ffn_b[l]))

        attn_l, k_l, v_l, ki_l, u_l = [], [], [], [], []
        for b in range(bsz):
            q, qi, u, ws, kt, vt, kt16, vt16, kit, kit16 = _proj_call(
                xp[b * seq:(b + 1) * seq], *proj_w, d_attn, d_ssm, pre_ln, _pick_tile(seq, PROJ_ROWS))
            attn_l.append(_pattn_call(rel_bias, q[None], qi[None], ws[None], kit16[None], kt16[None], vt16[None],
                                      tq=QUERY_BLOCK, kb=KEY_CHUNK, scan_kb=SCAN_CHUNK)[0])
            k_l.append(jnp.transpose(kt.reshape(N_HEADS, HEAD_DIM, seq), (2, 0, 1)))
            v_l.append(jnp.transpose(vt.reshape(N_HEADS, HEAD_DIM, seq), (2, 0, 1)))
            ki_l.append(kit.T)
            u_l.append(u)
        u_tm = jnp.stack(u_l, axis=1).reshape(seq * bsz, d_ssm)
        h0 = jnp.zeros((bsz, n_state), F32)
        ssm_tm, hr, hi = _ssm_call(u_tm, h0, h0, *ssm_w, nb=bsz, tt=_pick_tile(seq, SSM_ROWS) * bsz)
        ssm_o = jnp.swapaxes(ssm_tm.reshape(seq, bsz, d_ssm), 0, 1).reshape(bsz * seq, d_ssm)
        xp = _finish_call(xp, jnp.concatenate(attn_l, axis=0), ssm_o, *fin_w, alpha=alpha, pre_ln=pre_ln,
                          tm=_pick_tile(bsz * seq, FINISH_ROWS))
        outs[0].append(jnp.stack(k_l))
        outs[1].append(jnp.stack(v_l))
        outs[2].append(jnp.stack(ki_l))
        outs[3].append(hr.reshape(bsz, n_grp, SSM_STATE).astype(state_ssm_re.dtype))
        outs[4].append(hi.reshape(bsz, n_grp, SSM_STATE).astype(state_ssm_im.dtype))

        n_s = dbsz * dseq
        q, qi, u, ws, kt, vt, kt16, vt16, kit, kit16 = _proj_call(xs, *proj_w, d_attn, d_ssm, pre_ln,
                                                                  _pick_tile(n_s, PROJ_ROWS))
        rows = N_IDX_HEADS * dseq
        qis = jnp.swapaxes(qi.reshape(dbsz, dseq, N_IDX_HEADS, IDX_DIM), 1, 2).reshape(dbsz, rows, IDX_DIM)
        wis = jnp.swapaxes(ws[:, :N_IDX_HEADS].reshape(dbsz, dseq, N_IDX_HEADS), 1, 2).reshape(dbsz, rows, 1)
        per_seq_t = lambda a: _pad_lanes(jnp.swapaxes(a.reshape(a.shape[0], dbsz, dseq), 0, 1))
        ck_t = jnp.transpose(cache_k[l], (0, 2, 3, 1)).reshape(n_pool, d_attn, page)
        cv_t = jnp.transpose(cache_v[l], (0, 2, 3, 1)).reshape(n_pool, d_attn, page)
        cki_t = jnp.swapaxes(cache_kidx[l], 1, 2)
        attn = _sattn_call(page_table, rel_bias, q.reshape(dbsz, dseq, d_attn), qis, wis,
                           per_seq_t(kit16[:IDX_DIM]), per_seq_t(kt16), per_seq_t(vt16), cki_t, ck_t, cv_t,
                           cp=min(PAGES_PER_CHUNK, page_table.shape[1]))
        u_tm = jnp.swapaxes(u.reshape(dbsz, dseq, d_ssm), 0, 1).reshape(n_s, d_ssm)
        ssm_tm, hr, hi = _ssm_call(u_tm, state_ssm_re[l].reshape(dbsz, n_state).astype(F32),
                                   state_ssm_im[l].reshape(dbsz, n_state).astype(F32), *ssm_w, nb=dbsz, tt=n_s)
        ssm_o = jnp.swapaxes(ssm_tm.reshape(dseq, dbsz, d_ssm), 0, 1).reshape(n_s, d_ssm)
        xs = _finish_call(xs, attn.reshape(n_s, d_attn), ssm_o, *fin_w, alpha=alpha, pre_ln=pre_ln,
                          tm=_pick_tile(n_s, FINISH_ROWS))
        outs[5].append(kt.T.reshape(dbsz, dseq, N_HEADS, HEAD_DIM))
        outs[6].append(vt.T.reshape(dbsz, dseq, N_HEADS, HEAD_DIM))
        outs[7].append(kit.T.reshape(dbsz, dseq, IDX_DIM))
        outs[8].append(hr.reshape(dbsz, n_grp, SSM_STATE).astype(state_ssm_re.dtype))
        outs[9].append(hi.reshape(dbsz, n_grp, SSM_STATE).astype(state_ssm_im.dtype))

    return (xp.reshape(bsz, seq, d_model), xs.reshape(dbsz, dseq, d_model)) + tuple(jnp.stack(o) for o in outs)
```

```python
import functools
import math

import numpy as np
import jax
import jax.numpy as jnp
from jax import lax
from jax.experimental import pallas as pl
from jax.experimental.pallas import tpu as pltpu

N_HEADS = 8
HEAD_DIM = 64
N_IDX_HEADS = 8
IDX_DIM = 64
TOPK_MAX = 256
NUM_BUCKETS = 32
MAX_DISTANCE = 128
SSM_GROUP = 16
SSM_STATE = 64
N_EXPERT_GROUPS = 4
EXPERTS_PER_GROUP = 4
N_EXPERTS = N_EXPERT_GROUPS * EXPERTS_PER_GROUP
LN_EPS = 1e-5
NEG_INF = -1e30
LOG2E = math.log2(math.e)

LANES = 128
VMEM_LIMIT = 56 * 1024 * 1024

PROJ_ROWS = 512
FINISH_ROWS = 512
SSM_ROWS = 512
QUERY_BLOCK = LANES
KEY_CHUNK = 1024
SCAN_CHUNK = 512
PAGES_PER_CHUNK = 16
KV_SLOTS = 4
N_BISECT = 16
N_COARSE = 10
N_CAND = 3
N_FINE = 8
SHIFT_UNDERFLOW = 2.0 ** -100

F32 = jnp.float32
BF16 = jnp.bfloat16
_NT = (((1,), (1,)), ((), ()))


def _cparams(n_grid, vmem=VMEM_LIMIT):
    return pltpu.CompilerParams(dimension_semantics=("arbitrary",) * n_grid, vmem_limit_bytes=vmem)


def _resident(shape):
    nd = len(shape)
    return pl.BlockSpec(shape, lambda *_: (0,) * nd, pipeline_mode=pl.Buffered(1))


def _layer_norm(x, g, b):
    mu = jnp.mean(x, axis=-1, keepdims=True)
    xc = x - mu
    var = jnp.mean(xc * xc, axis=-1, keepdims=True)
    return xc * lax.rsqrt(var + LN_EPS) * g + b


def _proj_kernel(x_ref, g_ref, b_ref, wr_ref, wt_ref, ig_ref, ib_ref,
                 q_ref, qi_ref, u_ref, ws_ref, kt_ref, vt_ref, ktb_ref, vtb_ref, kit_ref, kitb_ref,
                 *, d_attn, d_ssm, pre_ln):
    xn = x_ref[...]
    if pre_ln:
        xn = _layer_norm(xn, g_ref[...], b_ref[...])
    x16 = xn.astype(BF16)
    d_qi = N_IDX_HEADS * IDX_DIM
    hr = jnp.dot(x16, wr_ref[...], preferred_element_type=F32)
    ht = lax.dot_general(wt_ref[...], x16, _NT, preferred_element_type=F32)
    q_ref[...] = (hr[:, :d_attn] * (HEAD_DIM ** -0.5 * LOG2E)).astype(BF16)
    qi_ref[...] = (hr[:, d_attn:d_attn + d_qi] * IDX_DIM ** -0.5).astype(BF16)
    u_ref[...] = hr[:, d_attn + d_qi:d_attn + d_qi + d_ssm]
    ws_ref[...] = hr[:, d_attn + d_qi + d_ssm:] * N_IDX_HEADS ** -0.5
    kt = ht[:d_attn]
    vt = ht[d_attn:2 * d_attn]
    kt_ref[...] = kt
    vt_ref[...] = vt
    ktb_ref[...] = kt.astype(BF16)
    vtb_ref[...] = vt.astype(BF16)
    ki = ht[2 * d_attn:]
    mu = jnp.mean(ki, axis=0, keepdims=True)
    kc = ki - mu
    var = jnp.mean(kc * kc, axis=0, keepdims=True)
    kin = kc * lax.rsqrt(var + LN_EPS) * ig_ref[...] + ib_ref[...]
    kit_ref[...] = kin
    kin16 = kin.astype(BF16)
    kitb_ref[...] = jnp.concatenate([kin16, kin16], axis=0)


def _proj_call(x, ln_g, ln_b, w_row, w_t, ig, ib, d_attn, d_ssm, pre_ln, tm):
    n, d = x.shape
    d_qi = N_IDX_HEADS * IDX_DIM
    row = lambda c, dt: (jax.ShapeDtypeStruct((n, c), dt), pl.BlockSpec((tm, c), lambda i: (i, 0)))
    col = lambda r, dt: (jax.ShapeDtypeStruct((r, n), dt), pl.BlockSpec((r, tm), lambda i: (0, i)))
    outs = [row(d_attn, BF16), row(d_qi, BF16), row(d_ssm, F32), row(LANES, F32),
            col(d_attn, F32), col(d_attn, F32), col(d_attn, BF16), col(d_attn, BF16),
            col(IDX_DIM, F32), col(2 * IDX_DIM, BF16)]
    return pl.pallas_call(
        functools.partial(_proj_kernel, d_attn=d_attn, d_ssm=d_ssm, pre_ln=pre_ln),
        grid=(n // tm,),
        in_specs=[pl.BlockSpec((tm, d), lambda i: (i, 0)), _resident((1, d)), _resident((1, d)),
                  _resident(w_row.shape), _resident(w_t.shape), _resident((IDX_DIM, 1)), _resident((IDX_DIM, 1))],
        out_specs=tuple(o[1] for o in outs),
        out_shape=tuple(o[0] for o in outs),
        compiler_params=_cparams(1),
        name="proj",
    )(x, ln_g, ln_b, w_row, w_t, ig, ib)


def _rel_bucket(dist):
    n = jnp.maximum(dist, 0)
    max_exact = NUM_BUCKETS // 2
    scaled = jnp.log(jnp.maximum(n, 1).astype(F32) / max_exact) / math.log(MAX_DISTANCE / max_exact)
    large = jnp.minimum(max_exact + jnp.floor(scaled * (NUM_BUCKETS - max_exact)).astype(jnp.int32),
                        NUM_BUCKETS - 1)
    return jnp.where(n < max_exact, n, large)


def _shifted_bias(dist, rb_ref, h):
    bkt = _rel_bucket(dist)
    far = rb_ref[NUM_BUCKETS - 1, h]
    out = jnp.zeros(dist.shape, F32)
    for b in range(NUM_BUCKETS - 1):
        out = jnp.where(bkt == b, rb_ref[b, h] - far, out)
    return out * LOG2E


def _lane_tile_sum(x):
    acc = x[:, :LANES]
    for t in range(1, x.shape[1] // LANES):
        acc = acc + x[:, t * LANES:(t + 1) * LANES]
    return acc


def _lane_tile_max(x):
    acc = x[:, :LANES]
    for t in range(1, x.shape[1] // LANES):
        acc = jnp.maximum(acc, x[:, t * LANES:(t + 1) * LANES])
    return acc


def _lane_tile_min(x):
    acc = x[:, :LANES]
    for t in range(1, x.shape[1] // LANES):
        acc = jnp.minimum(acc, x[:, t * LANES:(t + 1) * LANES])
    return acc


def _chunk_of(ref, kb):
    return lambda c: ref[:, pl.ds(pl.multiple_of(c * kb, kb), kb)]


def _count_ge(ref, nch, kb, t):
    chunk = _chunk_of(ref, kb)

    def body(c, cnt):
        return cnt + _lane_tile_sum(jnp.where(chunk(c) >= t, 1.0, 0.0))
    cnt = lax.fori_loop(0, nch, body, jnp.zeros((ref.shape[0], LANES), F32))
    return jnp.sum(cnt, axis=-1, keepdims=True)


def _kth_largest(ref, nch, kb, k_row, rmin, rmax, n_bisect):
    rows = ref.shape[0]
    chunk = _chunk_of(ref, kb)

    def bisect(_, lohi):
        lo, hi = lohi
        mid = 0.5 * (lo + hi)
        ok = _count_ge(ref, nch, kb, mid) >= k_row
        return jnp.where(ok, mid, lo), jnp.where(ok, hi, mid)

    _, hi = lax.fori_loop(0, n_bisect, bisect, (rmin, rmax))

    def count_ge_and_next(m):
        def body(c, carry):
            cnt, nxt = carry
            x = chunk(c)
            cnt = cnt + _lane_tile_sum(jnp.where(x >= m, 1.0, 0.0))
            nxt = jnp.maximum(nxt, _lane_tile_max(jnp.where(x < m, x, NEG_INF)))
            return cnt, nxt
        cnt, nxt = lax.fori_loop(0, nch, body, (jnp.zeros((rows, LANES), F32),
                                                jnp.full((rows, LANES), NEG_INF, F32)))
        return jnp.sum(cnt, axis=-1, keepdims=True), jnp.max(nxt, axis=-1, keepdims=True)

    def first_le(t):
        def body(c, nxt):
            x = chunk(c)
            return jnp.maximum(nxt, _lane_tile_max(jnp.where(x <= t, x, NEG_INF)))
        nxt = lax.fori_loop(0, nch, body, jnp.full((rows, LANES), NEG_INF, F32))
        return jnp.max(nxt, axis=-1, keepdims=True)

    def walk_cond(st):
        return st[3] > 0.0

    def walk_body(st):
        m, cfin, done, _ = st
        cnt, nxt = count_ge_and_next(m)
        ok = jnp.logical_and(cnt >= k_row, done < 0.5)
        cfin = jnp.where(ok, cnt, cfin)
        done = jnp.where(cnt >= k_row, 1.0, done)
        m = jnp.where(done > 0.5, m, nxt)
        return m, cfin, done, jnp.sum(1.0 - done)

    zeros = jnp.zeros((rows, 1), F32)
    thr, cfin, _, _ = lax.while_loop(walk_cond, walk_body, (first_le(hi), zeros, zeros, jnp.float32(rows)))
    return thr, cfin


def _drop_ties(ref, nch, kb, thr, k_row, n_index_bits):
    rows = ref.shape[0]
    chunk = _chunk_of(ref, kb)

    def count_gt():
        def body(c, cnt):
            return cnt + _lane_tile_sum(jnp.where(chunk(c) > thr, 1.0, 0.0))
        cnt = lax.fori_loop(0, nch, body, jnp.zeros((rows, LANES), F32))
        return jnp.sum(cnt, axis=-1, keepdims=True)

    need = k_row - count_gt()

    def count_eq_upto(j):
        def body(c, cnt):
            x = chunk(c)
            idx = (c * kb + lax.broadcasted_iota(jnp.int32, x.shape, 1)).astype(F32)
            hit = jnp.where(x == thr, jnp.where(idx <= j, 1.0, 0.0), 0.0)
            return cnt + _lane_tile_sum(hit)
        cnt = lax.fori_loop(0, nch, body, jnp.zeros((rows, LANES), F32))
        return jnp.sum(cnt, axis=-1, keepdims=True)

    def ibisect(_, lohi):
        lo, hi_i = lohi
        mid = jnp.floor(0.5 * (lo + hi_i))
        ok = count_eq_upto(mid) >= need
        return jnp.where(ok, lo, mid), jnp.where(ok, mid, hi_i)

    width = jnp.float32(1.0) * (nch * kb)
    _, cut = lax.fori_loop(0, n_index_bits, ibisect,
                           (jnp.full((rows, 1), -1.0, F32), jnp.zeros((rows, 1), F32) + (width - 1.0)))

    def drop(c, _):
        off = pl.multiple_of(c * kb, kb)
        x = ref[:, pl.ds(off, kb)]
        idx = (c * kb + lax.broadcasted_iota(jnp.int32, x.shape, 1)).astype(F32)
        dropped = jnp.where(x == thr, jnp.where(idx > cut, 1.0, 0.0), 0.0)
        ref[:, pl.ds(off, kb)] = jnp.where(dropped > 0.5, NEG_INF, x)
        return 0
    lax.fori_loop(0, nch, drop, 0)


def _topk_threshold(sc_ref, nch, kb, k_row, rmin, rmax, n_bisect, n_index_bits):
    thr, cfin = _kth_largest(sc_ref, nch, kb, k_row, rmin, rmax, n_bisect)

    @pl.when(jnp.sum(jnp.where(cfin > k_row, 1.0, 0.0)) > 0.0)
    def _():
        _drop_ties(sc_ref, nch, kb, thr, k_row, n_index_bits)

    return thr


def _topk_threshold_two_level(sc_ref, cand_ref, res_ref, nch, kb, k_row, n_valid, rmin, rmax,
                              n_coarse, n_bisect, n_index_bits):
    rows = sc_ref.shape[0]
    chunk = _chunk_of(sc_ref, kb)
    hi0 = rmax + jnp.maximum(jnp.abs(rmax), 1e-30) * 2.0 ** -20

    def bisect(_, st):
        lo, hi, c_lo, c_hi = st
        mid = 0.5 * (lo + hi)
        c = _count_ge(sc_ref, nch, kb, mid)
        ok = c >= k_row
        return (jnp.where(ok, mid, lo), jnp.where(ok, hi, mid), jnp.where(ok, c, c_lo), jnp.where(ok, c_hi, c))

    lo, hi, c_lo, c_hi = lax.fori_loop(0, n_coarse, bisect, (rmin, hi0, n_valid, jnp.zeros((rows, 1), F32)))

    n_cand = cand_ref.shape[1] // LANES

    def capture(c, top):
        x = chunk(c)
        x = jnp.where(x < hi, x, NEG_INF)
        top = list(top)
        for t in range(kb // LANES):
            v = x[:, t * LANES:(t + 1) * LANES]
            for r in range(n_cand - 1):
                top[r], v = jnp.maximum(top[r], v), jnp.minimum(top[r], v)
            top[-1] = jnp.maximum(top[-1], v)
        return tuple(top)

    neg = jnp.full((rows, LANES), NEG_INF, F32)
    top = lax.fori_loop(0, nch, capture, (neg,) * n_cand)
    for r in range(n_cand):
        cand_ref[:, r * LANES:(r + 1) * LANES] = top[r]
    kept = _count_ge(cand_ref, 1, n_cand * LANES, lo)
    missed = jnp.sum(jnp.where(kept == c_lo - c_hi, 0.0, 1.0))

    @pl.when(missed == 0.0)
    def _():
        thr, cnt = _kth_largest(cand_ref, 1, n_cand * LANES, k_row - c_hi, lo, hi, N_FINE)
        res_ref[:, 0:LANES] = jnp.broadcast_to(thr, (rows, LANES))
        res_ref[:, LANES:2 * LANES] = jnp.broadcast_to(cnt + c_hi, (rows, LANES))

    @pl.when(missed > 0.0)
    def _():
        thr, cnt = _kth_largest(sc_ref, nch, kb, k_row, rmin, rmax, n_bisect)
        res_ref[:, 0:LANES] = jnp.broadcast_to(thr, (rows, LANES))
        res_ref[:, LANES:2 * LANES] = jnp.broadcast_to(cnt, (rows, LANES))

    thr = res_ref[:, 0:1]
    cfin = res_ref[:, LANES:LANES + 1]

    @pl.when(jnp.sum(jnp.where(cfin > k_row, 1.0, 0.0)) > 0.0)
    def _():
        _drop_ties(sc_ref, nch, kb, thr, k_row, n_index_bits)

    return thr


def _pattn_kernel(rb_ref, q_ref, qi_ref, ws_ref, kit_ref, kt_ref, vt_ref, o_ref,
                  sc_ref, cand_ref, res_ref, qm_ref, qim_ref, bias_ref, kn_ref, m_ref, l_ref, acc_ref,
                  *, tq, kb, scan_kb, n_bisect, n_index_bits):
    i = pl.program_id(1)
    n_pairs = N_HEADS // 2
    nch = ((i + 1) * tq + kb - 1) // kb
    sub = kb // tq
    lane = lax.broadcasted_iota(jnp.int32, (tq, LANES), 1)
    lo_half = lane < HEAD_DIM
    q_pos = i * tq + lax.broadcasted_iota(jnp.int32, (tq, 1), 0)

    @pl.when(i == 0)
    def _():
        r = lax.broadcasted_iota(jnp.int32, (tq, tq), 0)
        c = lax.broadcasted_iota(jnp.int32, (tq, tq), 1)
        for h in range(N_HEADS):
            bias_ref[0, h] = jnp.zeros((tq, tq), F32)
            bias_ref[1, h] = _shifted_bias(r - c + tq, rb_ref, h)
            bias_ref[2, h] = _shifted_bias(r - c, rb_ref, h)

        for h in range(N_HEADS):
            def sq_norm_max(c, mx):
                x = kt_ref[h * HEAD_DIM:(h + 1) * HEAD_DIM, pl.ds(pl.multiple_of(c * kb, kb), kb)].astype(F32)
                return jnp.maximum(mx, jnp.sum(x * x, axis=0, keepdims=True))
            mx = lax.fori_loop(0, kt_ref.shape[1] // kb, sq_norm_max, jnp.zeros((1, kb), F32))
            kn_ref[h] = jnp.broadcast_to(jnp.sqrt(jnp.max(mx, axis=-1, keepdims=True)), (1, LANES))

    for h in range(N_HEADS):
        keep = lo_half if h % 2 == 0 else jnp.logical_not(lo_half)
        pr = h // 2
        qm_ref[h] = jnp.where(keep, q_ref[:, pr * LANES:(pr + 1) * LANES].astype(F32), 0.0).astype(BF16)
        qim_ref[h] = jnp.where(keep, qi_ref[:, pr * LANES:(pr + 1) * LANES].astype(F32), 0.0).astype(BF16)

    wcols = [ws_ref[:, h:h + 1] for h in range(N_IDX_HEADS)]

    def score_chunk(c, carry):
        rmin, rmax = carry
        off = pl.multiple_of(c * kb, kb)
        kt = kit_ref[:, pl.ds(off, kb)]
        acc = jnp.zeros((tq, kb), F32)
        for h in range(N_IDX_HEADS):
            s = jnp.dot(qim_ref[h], kt, preferred_element_type=F32)
            acc = acc + jnp.maximum(s, 0.0) * wcols[h]
        key_pos = off + lax.broadcasted_iota(jnp.int32, (tq, kb), 1)
        valid = key_pos <= q_pos
        sc_ref[:, pl.ds(off, kb)] = jnp.where(valid, acc, NEG_INF)
        rmax = jnp.maximum(rmax, _lane_tile_max(jnp.where(valid, acc, NEG_INF)))
        rmin = jnp.minimum(rmin, _lane_tile_min(jnp.where(valid, acc, -NEG_INF)))
        return rmin, rmax

    rmin, rmax = lax.fori_loop(0, nch, score_chunk,
                               (jnp.full((tq, LANES), -NEG_INF, F32), jnp.full((tq, LANES), NEG_INF, F32)))
    rmin = jnp.min(rmin, axis=-1, keepdims=True)
    rmax = jnp.max(rmax, axis=-1, keepdims=True)

    topk = min(TOPK_MAX, kit_ref.shape[1] // 4)
    k_row = jnp.minimum(q_pos + 1, topk).astype(F32)
    nscan = ((i + 1) * tq + scan_kb - 1) // scan_kb
    thr = _topk_threshold_two_level(sc_ref, cand_ref, res_ref, nscan, scan_kb, k_row, (q_pos + 1).astype(F32),
                                    rmin, rmax, N_COARSE, n_bisect, n_index_bits)

    thr_row = jnp.concatenate([jnp.broadcast_to(thr, (tq, LANES))] * (scan_kb // LANES), axis=1)

    def to_mask(c, _):
        off = pl.multiple_of(c * scan_kb, scan_kb)
        sc_ref[:, pl.ds(off, scan_kb)] = jnp.where(sc_ref[:, pl.ds(off, scan_kb)] >= thr_row, 0.0, NEG_INF)
        return 0
    lax.fori_loop(0, nscan, to_mask, 0)

    def logits(c, h, off, with_bias):
        pr = h // 2
        kp = kt_ref[pr * LANES:(pr + 1) * LANES, pl.ds(off, kb)]
        s = jnp.dot(qm_ref[h], kp, preferred_element_type=F32) + sc_ref[:, pl.ds(off, kb)]
        if with_bias:
            tiles = [bias_ref[jnp.clip(c * sub + t - i + 2, 0, 2), h] for t in range(sub)]
            s = s + jnp.concatenate(tiles, axis=1)
        return s

    def max_chunk(c, with_bias):
        off = pl.multiple_of(c * kb, kb)
        for h in range(N_HEADS):
            m_ref[h] = jnp.maximum(m_ref[h], _lane_tile_max(logits(c, h, off, with_bias)))

    def acc_chunk(c, with_bias):
        off = pl.multiple_of(c * kb, kb)
        for pr in range(n_pairs):
            vp = vt_ref[pr * LANES:(pr + 1) * LANES, pl.ds(off, kb)]
            for hh in range(2):
                h = 2 * pr + hh
                p = jnp.exp2(logits(c, h, off, with_bias) - jnp.concatenate([m_ref[h]] * sub, axis=1))
                l_ref[h] = l_ref[h] + _lane_tile_sum(p)
                pv = lax.dot_general(p.astype(BF16), vp, _NT, preferred_element_type=F32)
                mine = lo_half if hh == 0 else jnp.logical_not(lo_half)
                acc_ref[pr] = acc_ref[pr] + jnp.where(mine, pv, 0.0)

    def sweep(fn):
        n_far = jnp.maximum((i * tq - MAX_DISTANCE + 1) // kb, 0)

        def far_body(c, _):
            fn(c, False)
            return 0

        def near_body(c, _):
            fn(c, True)
            return 0

        lax.fori_loop(0, n_far, far_body, 0)
        lax.fori_loop(n_far, nch, near_body, 0)

    def accumulate():
        l_ref[...] = jnp.zeros(l_ref.shape, F32)
        acc_ref[...] = jnp.zeros(acc_ref.shape, F32)
        sweep(acc_chunk)

    for h in range(N_HEADS):
        bias_max = jnp.float32(0.0)
        for bkt in range(NUM_BUCKETS - 1):
            bias_max = jnp.maximum(bias_max, rb_ref[bkt, h] - rb_ref[NUM_BUCKETS - 1, h])
        qf = qm_ref[h].astype(F32)
        q_norm = jnp.sqrt(jnp.sum(qf * qf, axis=-1, keepdims=True))
        m_ref[h] = jnp.broadcast_to(q_norm * kn_ref[h][:, :1] + bias_max * LOG2E, (tq, LANES))
    accumulate()
    l_min = jnp.min(jnp.sum(l_ref[0], axis=-1, keepdims=True))
    for h in range(1, N_HEADS):
        l_min = jnp.minimum(l_min, jnp.min(jnp.sum(l_ref[h], axis=-1, keepdims=True)))

    @pl.when(jnp.logical_not(l_min >= SHIFT_UNDERFLOW))
    def _():
        m_ref[...] = jnp.full(m_ref.shape, NEG_INF, F32)
        sweep(max_chunk)
        for h in range(N_HEADS):
            m_ref[h] = jnp.broadcast_to(jnp.max(m_ref[h], axis=-1, keepdims=True), (tq, LANES))
        accumulate()

    for pr in range(n_pairs):
        l0 = jnp.sum(l_ref[2 * pr], axis=-1, keepdims=True)
        l1 = jnp.sum(l_ref[2 * pr + 1], axis=-1, keepdims=True)
        o_ref[:, pr * LANES:(pr + 1) * LANES] = acc_ref[pr] / jnp.where(lo_half, l0, l1)


def _pattn_call(rel_bias, q, qi, ws, kit, kt16, vt16, tq, kb, scan_kb):
    bsz, s, d_attn = q.shape
    assert tq == LANES and MAX_DISTANCE <= tq and s % kb == 0 and kb % scan_kb == 0 and scan_kb % tq == 0
    n_index_bits = int(math.ceil(math.log2(s))) + 1
    blk = lambda c: pl.BlockSpec((None, tq, c), lambda b, i: (b, i, 0))
    whole = lambda r, c: pl.BlockSpec((None, r, c), lambda b, i: (b, 0, 0), pipeline_mode=pl.Buffered(1))
    return pl.pallas_call(
        functools.partial(_pattn_kernel, tq=tq, kb=kb, scan_kb=scan_kb, n_bisect=N_BISECT,
                          n_index_bits=n_index_bits),
        grid=(bsz, s // tq),
        in_specs=[pl.BlockSpec(memory_space=pltpu.SMEM),
                  blk(d_attn), blk(N_IDX_HEADS * IDX_DIM), blk(LANES),
                  whole(2 * IDX_DIM, s), whole(d_attn, s), whole(d_attn, s)],
        out_specs=blk(d_attn),
        out_shape=jax.ShapeDtypeStruct((bsz, s, d_attn), F32),
        scratch_shapes=[
            pltpu.VMEM((tq, s), F32),
            pltpu.VMEM((tq, N_CAND * LANES), F32),
            pltpu.VMEM((tq, 2 * LANES), F32),
            pltpu.VMEM((N_HEADS, tq, LANES), BF16),
            pltpu.VMEM((N_IDX_HEADS, tq, LANES), BF16),
            pltpu.VMEM((3, N_HEADS, tq, tq), F32),
            pltpu.VMEM((N_HEADS, 1, LANES), F32),
            pltpu.VMEM((N_HEADS, tq, LANES), F32),
            pltpu.VMEM((N_HEADS, tq, LANES), F32),
            pltpu.VMEM((N_HEADS // 2, tq, LANES), F32),
        ],
        compiler_params=_cparams(2),
        name="pattn",
    )(rel_bias, q, qi, ws, kit, kt16, vt16)


def _sattn_kernel(pt_ref, rb_ref, q_ref, qis_ref, w_ref, kin_ref, kn_ref, vn_ref,
                  cki_hbm, ck_hbm, cv_hbm, o_ref,
                  sc_ref, ibuf, kbuf, vbuf, isem, ksem, vsem, *, cp, page, n_bisect, n_index_bits):
    b = pl.program_id(0)
    n_pages = pt_ref.shape[1]
    nc = n_pages // cp
    ck = cp * page
    past = n_pages * page
    sn = q_ref.shape[0]
    rows = N_HEADS * sn
    d_attn = q_ref.shape[1]

    n_b = pl.num_programs(0)
    i_slots, kv_slots = ibuf.shape[0], kbuf.shape[0]

    def page_copies(cache, buf, sem, seq):
        row, c, slot = b + seq // nc, seq % nc, seq % buf.shape[0]
        return [pltpu.make_async_copy(cache.at[pt_ref[row, c * cp + j]],
                                      buf.at[slot, j], sem.at[slot])
                for j in range(cp)]

    def start(streams, seq):
        def go():
            for prio, (cache, buf, sem) in enumerate(streams):
                for cpy in page_copies(cache, buf, sem, seq):
                    cpy.start(priority=prio % 2)
        if seq < nc:
            go()
        else:
            pl.when(b + 1 < n_b)(go)

    def wait(streams, seq):
        for cache, buf, sem in streams:
            for cpy in page_copies(cache, buf, sem, seq):
                cpy.wait()

    def chunk_of(buf, slot):
        return jnp.concatenate([buf[slot, j].astype(BF16) for j in range(cp)], axis=1)

    idx_stream = [(cki_hbm, ibuf, isem)]
    kv_streams = [(ck_hbm, kbuf, ksem), (cv_hbm, vbuf, vsem)]
    i_ahead, kv_ahead = i_slots - 1, kv_slots - 1

    @pl.when(b == 0)
    def _():
        for seq in range(i_ahead):
            start(idx_stream, seq)
        for seq in range(kv_ahead):
            start(kv_streams, seq)

    q_pos = past + lax.broadcasted_iota(jnp.int32, (sn, 1), 0)
    qis = qis_ref[...]
    w = w_ref[...]

    def head_sum(s):
        acc = jnp.zeros((sn, s.shape[1]), F32)
        for h in range(N_IDX_HEADS):
            acc = acc + jnp.maximum(s[h * sn:(h + 1) * sn], 0.0) * w[h * sn:(h + 1) * sn]
        return acc

    rmin = jnp.full((sn, LANES), -NEG_INF, F32)
    rmax = jnp.full((sn, LANES), NEG_INF, F32)
    for c in range(nc):
        slot = c % i_slots
        start(idx_stream, c + i_ahead)
        wait(idx_stream, c)
        acc = head_sum(jnp.dot(qis, chunk_of(ibuf, slot), preferred_element_type=F32))
        sc_ref[:, c * ck:(c + 1) * ck] = acc
        rmax = jnp.maximum(rmax, _lane_tile_max(acc))
        rmin = jnp.minimum(rmin, _lane_tile_min(acc))
    acc = head_sum(jnp.dot(qis, kin_ref[...], preferred_element_type=F32))
    lane = lax.broadcasted_iota(jnp.int32, (sn, LANES), 1)
    valid = jnp.logical_and(lane < sn, past + lane <= q_pos)
    sc_ref[:, past:past + LANES] = jnp.where(valid, acc, NEG_INF)
    sc_ref[:, past + LANES:past + ck] = jnp.full((sn, ck - LANES), NEG_INF, F32)
    rmax = jnp.max(jnp.maximum(rmax, jnp.where(valid, acc, NEG_INF)), axis=-1, keepdims=True)
    rmin = jnp.min(jnp.minimum(rmin, jnp.where(valid, acc, -NEG_INF)), axis=-1, keepdims=True)

    topk = min(TOPK_MAX, (past + sn) // 4)
    k_row = jnp.minimum(q_pos + 1, topk).astype(F32)
    thr = _topk_threshold(sc_ref, nc + 1, ck, k_row, rmin, rmax, n_bisect, n_index_bits)

    r_head = lax.broadcasted_iota(jnp.int32, (rows, d_attn), 0) // sn
    c_head = lax.broadcasted_iota(jnp.int32, (rows, d_attn), 1) // HEAD_DIM
    own = r_head == c_head
    qbd = jnp.where(own, jnp.tile(q_ref[...].astype(F32), (N_HEADS, 1)), 0.0).astype(BF16)

    def softmax_step(state, s, vals_t):
        m_prev, l_prev, acc_prev = state
        m_next = jnp.maximum(m_prev, jnp.max(s, axis=-1, keepdims=True))
        alpha = jnp.exp2(m_prev - m_next)
        p = jnp.exp2(s - m_next[:, :1])
        l_next = alpha * l_prev + _lane_tile_sum(p)
        pv = lax.dot_general(p.astype(BF16), vals_t, _NT, preferred_element_type=F32)
        return m_next, l_next, acc_prev * alpha[:, :1] + pv

    def biased(s, key0):
        dist = q_pos - (key0 + lax.broadcasted_iota(jnp.int32, (sn, s.shape[1]), 1))
        return s + jnp.concatenate([_shifted_bias(dist, rb_ref, h) for h in range(N_HEADS)], axis=0)

    state = (jnp.full((rows, LANES), NEG_INF, F32), jnp.zeros((rows, LANES), F32),
             jnp.zeros((rows, d_attn), F32))
    for c in range(nc):
        slot = c % kv_slots
        start(kv_streams, c + kv_ahead)
        wait(kv_streams, c)
        madd = jnp.where(sc_ref[:, c * ck:(c + 1) * ck] >= thr, 0.0, NEG_INF)
        s = jnp.dot(qbd, chunk_of(kbuf, slot), preferred_element_type=F32)
        s = s + jnp.tile(madd, (N_HEADS, 1))
        if (c + 1) * ck + MAX_DISTANCE > past:
            s = biased(s, c * ck)
        state = softmax_step(state, s, chunk_of(vbuf, slot))
    madd = jnp.where(sc_ref[:, past:past + LANES] >= thr, 0.0, NEG_INF)
    s = jnp.dot(qbd, kn_ref[...], preferred_element_type=F32)
    s = biased(s + jnp.tile(madd, (N_HEADS, 1)), past)
    _, l_fin, acc = softmax_step(state, s, vn_ref[...])
    res = jnp.where(own, acc / jnp.sum(l_fin, axis=-1, keepdims=True), 0.0)
    out = res[0:sn]
    for h in range(1, N_HEADS):
        out = out + res[h * sn:(h + 1) * sn]
    o_ref[...] = out


def _sattn_call(page_table, rel_bias, q, qis, w, kin_t, kn_t, vn_t, cache_kidx_t, cache_k_t, cache_v_t, cp):
    db, sn, d_attn = q.shape
    n_pages = page_table.shape[1]
    page = cache_k_t.shape[2]
    assert n_pages % cp == 0 and sn == 8 and page == LANES and cp >= 2 and MAX_DISTANCE <= cp * page
    ck = cp * page
    rows = N_HEADS * sn
    nc = n_pages // cp
    kv_slots = math.gcd(nc, KV_SLOTS)
    assert nc % 2 == 0 and kv_slots >= 2
    n_index_bits = int(math.ceil(math.log2(n_pages * page + ck))) + 1
    per_b = lambda r, c: pl.BlockSpec((None, r, c), lambda b, pt: (b, 0, 0))
    grid_spec = pltpu.PrefetchScalarGridSpec(
        num_scalar_prefetch=1,
        grid=(db,),
        in_specs=[pl.BlockSpec(memory_space=pltpu.SMEM),
                  per_b(sn, d_attn), per_b(rows, IDX_DIM), per_b(rows, 1),
                  per_b(IDX_DIM, LANES), per_b(d_attn, LANES), per_b(d_attn, LANES),
                  pl.BlockSpec(memory_space=pl.ANY), pl.BlockSpec(memory_space=pl.ANY),
                  pl.BlockSpec(memory_space=pl.ANY)],
        out_specs=per_b(sn, d_attn),
        scratch_shapes=[
            pltpu.VMEM((sn, n_pages * page + ck), F32),
            pltpu.VMEM((2, cp, IDX_DIM, page), F32),
            pltpu.VMEM((kv_slots, cp, d_attn, page), F32),
            pltpu.VMEM((kv_slots, cp, d_attn, page), F32),
            pltpu.SemaphoreType.DMA((2,)), pltpu.SemaphoreType.DMA((kv_slots,)), pltpu.SemaphoreType.DMA((kv_slots,)),
        ],
    )
    return pl.pallas_call(
        functools.partial(_sattn_kernel, cp=cp, page=page, n_bisect=N_BISECT, n_index_bits=n_index_bits),
        grid_spec=grid_spec,
        out_shape=jax.ShapeDtypeStruct((db, sn, d_attn), F32),
        compiler_params=_cparams(1),
        name="sattn",
    )(page_table, rel_bias, q, qis, w, kin_t, kn_t, vn_t, cache_kidx_t, cache_k_t, cache_v_t)


def _gelu_tanh(x):
    return 0.5 * x * (1.0 + jnp.tanh(math.sqrt(2.0 / math.pi) * (x + 0.044715 * x * x * x)))


def _ssm_kernel(u_ref, h0r_ref, h0i_ref, lr_ref, li_ref, ldt_ref, bre_ref, bim_ref, cre_ref, cim_ref,
                d_ref, wglu_ref, o_ref, hr_out_ref, hi_out_ref,
                lbr_ref, lbi_ref, bbr_ref, bbi_ref, bur_ref, bui_ref, hre_ref, him_ref, sr_ref, si_ref,
                *, nb, unroll):
    step = pl.program_id(0)
    tt = u_ref.shape[0]
    d_ssm = u_ref.shape[1]

    @pl.when(step == 0)
    def _():
        lr, li = lr_ref[...], li_ref[...]
        dt = jnp.exp(ldt_ref[...])
        mag = jnp.exp(lr * dt)
        lbr, lbi = mag * jnp.cos(li * dt), mag * jnp.sin(li * dt)
        den = lr * lr + li * li
        xr, xi = lbr - 1.0, lbi
        cr = (xr * lr + xi * li) / den
        ci = (xi * lr - xr * li) / den
        lbr_ref[...] = lbr
        lbi_ref[...] = lbi
        bbr_ref[...] = (cr * bre_ref[...] - ci * bim_ref[...]).astype(BF16)
        bbi_ref[...] = (cr * bim_ref[...] + ci * bre_ref[...]).astype(BF16)
        sr_ref[...] = h0r_ref[...]
        si_ref[...] = h0i_ref[...]

    u = u_ref[...]
    u16 = u.astype(BF16)
    n_tiles = d_ssm // LANES
    st = lbr_ref.shape[1] // n_tiles
    for j in range(n_tiles):
        uj = u16[:, j * LANES:(j + 1) * LANES]
        bur_ref[:, j * st:(j + 1) * st] = jnp.dot(uj, bbr_ref[j * LANES:(j + 1) * LANES, j * st:(j + 1) * st],
                                                  preferred_element_type=F32)
        bui_ref[:, j * st:(j + 1) * st] = jnp.dot(uj, bbi_ref[j * LANES:(j + 1) * LANES, j * st:(j + 1) * st],
                                                  preferred_element_type=F32)
    lbr = jnp.broadcast_to(lbr_ref[...], (nb, lbr_ref.shape[1]))
    lbi = jnp.broadcast_to(lbi_ref[...], (nb, lbi_ref.shape[1]))

    def scan(t, carry):
        hr, hi = carry
        r0 = pl.multiple_of(t * nb, nb)
        nr = lbr * hr - lbi * hi + bur_ref[pl.ds(r0, nb), :]
        ni = lbr * hi + lbi * hr + bui_ref[pl.ds(r0, nb), :]
        hre_ref[pl.ds(r0, nb), :] = nr
        him_ref[pl.ds(r0, nb), :] = ni
        return nr, ni

    hr, hi = lax.fori_loop(0, tt // nb, scan, (sr_ref[...], si_ref[...]), unroll=unroll)
    sr_ref[...] = hr
    si_ref[...] = hi
    hr_out_ref[...] = hr
    hi_out_ref[...] = hi

    y_tiles = []
    for j in range(n_tiles):
        rows, cols = slice(j * st, (j + 1) * st), slice(j * LANES, (j + 1) * LANES)
        y_tiles.append(jnp.dot(hre_ref[:, rows].astype(BF16), cre_ref[rows, cols], preferred_element_type=F32)
                       - jnp.dot(him_ref[:, rows].astype(BF16), cim_ref[rows, cols], preferred_element_type=F32))
    y = jnp.concatenate(y_tiles, axis=1) + d_ref[...] * u
    z = _gelu_tanh(y)
    g = jnp.dot(z.astype(BF16), wglu_ref[...], preferred_element_type=F32)
    o_ref[...] = g[:, :d_ssm] / (1.0 + jnp.exp(-g[:, d_ssm:]))


def _ssm_call(u, h0r, h0i, lr, li, ldt, bre, bim, cre, cim, d, wglu, nb, tt):
    n, d_ssm = u.shape
    ns = lr.shape[1]
    assert n % tt == 0 and tt % nb == 0 and d_ssm % LANES == 0 and LANES % SSM_GROUP == 0
    unroll = 8 if (tt // nb) % 8 == 0 else 1
    state = jax.ShapeDtypeStruct((nb, ns), F32)
    return pl.pallas_call(
        functools.partial(_ssm_kernel, nb=nb, unroll=unroll),
        grid=(n // tt,),
        in_specs=[pl.BlockSpec((tt, d_ssm), lambda i: (i, 0)),
                  _resident((nb, ns)), _resident((nb, ns)),
                  _resident((1, ns)), _resident((1, ns)), _resident((1, ns)),
                  _resident((d_ssm, ns)), _resident((d_ssm, ns)),
                  _resident((ns, d_ssm)), _resident((ns, d_ssm)),
                  _resident((1, d_ssm)), _resident((d_ssm, 2 * d_ssm))],
        out_specs=(pl.BlockSpec((tt, d_ssm), lambda i: (i, 0)),
                   pl.BlockSpec((nb, ns), lambda i: (0, 0)), pl.BlockSpec((nb, ns), lambda i: (0, 0))),
        out_shape=(jax.ShapeDtypeStruct((n, d_ssm), F32), state, state),
        scratch_shapes=[
            pltpu.VMEM((1, ns), F32), pltpu.VMEM((1, ns), F32),
            pltpu.VMEM((d_ssm, ns), BF16), pltpu.VMEM((d_ssm, ns), BF16),
            pltpu.VMEM((tt, ns), F32), pltpu.VMEM((tt, ns), F32),
            pltpu.VMEM((tt, ns), F32), pltpu.VMEM((tt, ns), F32),
            pltpu.VMEM((nb, ns), F32), pltpu.VMEM((nb, ns), F32),
        ],
        compiler_params=_cparams(1),
        name="ssm",
    )(u, h0r, h0i, lr, li, ldt, bre, bim, cre, cim, d, wglu)


def _finish_kernel(x_ref, attn_ref, ssm_ref, gi_ref, bi_ref, wo_ref, g1_ref, b1_ref, wr_ref, br_ref,
                   wg_ref, wu_ref, wd_ref, g2_ref, b2_ref, o_ref, *, alpha, pre_ln):
    x = x_ref[...]
    if pre_ln:
        x = _layer_norm(x, gi_ref[...], bi_ref[...])
    d_attn = attn_ref.shape[1]
    mix = (jnp.dot(attn_ref[...].astype(BF16), wo_ref[:d_attn, :], preferred_element_type=F32)
           + jnp.dot(ssm_ref[...].astype(BF16), wo_ref[d_attn:, :], preferred_element_type=F32))
    x1 = _layer_norm(alpha * x + mix, g1_ref[...], b1_ref[...])
    x16 = x1.astype(BF16)

    lg = jnp.dot(x16, wr_ref[...], preferred_element_type=F32) + br_ref[...]
    lane = lax.broadcasted_iota(jnp.int32, lg.shape, 1)
    big = jnp.int32(1 << 20)
    is_grp = lane < N_EXPERT_GROUPS
    gl = jnp.where(is_grp, lg, NEG_INF)
    gmax = jnp.max(gl, axis=-1, keepdims=True)
    g_sel = jnp.min(jnp.where(gl == gmax, lane, big), axis=-1, keepdims=True)
    p_grp = 1.0 / jnp.sum(jnp.where(is_grp, jnp.exp(gl - gmax), 0.0), axis=-1, keepdims=True)
    e_idx = lane - N_EXPERT_GROUPS
    in_sel = jnp.logical_and(jnp.logical_and(e_idx >= 0, e_idx < N_EXPERTS),
                             e_idx // EXPERTS_PER_GROUP == g_sel)
    el = jnp.where(in_sel, lg, NEG_INF)
    v1 = jnp.max(el, axis=-1, keepdims=True)
    i1 = jnp.min(jnp.where(el == v1, lane, big), axis=-1, keepdims=True)
    el2 = jnp.where(lane == i1, NEG_INF, el)
    v2 = jnp.max(el2, axis=-1, keepdims=True)
    i2 = jnp.min(jnp.where(el2 == v2, lane, big), axis=-1, keepdims=True)
    e21 = jnp.exp(v2 - v1)
    w1 = p_grp / (1.0 + e21)
    w2 = p_grp * e21 / (1.0 + e21)
    gates = jnp.where(lane == i1, w1, jnp.where(lane == i2, w2, 0.0))

    y = jnp.zeros(x1.shape, F32)
    for e in range(N_EXPERTS):
        gate = gates[:, N_EXPERT_GROUPS + e:N_EXPERT_GROUPS + e + 1]
        hg = jnp.dot(x16, wg_ref[e], preferred_element_type=F32)
        hu = jnp.dot(x16, wu_ref[e], preferred_element_type=F32)
        hid = hg / (1.0 + jnp.exp(-hg)) * hu * gate
        y = y + jnp.dot(hid.astype(BF16), wd_ref[e], preferred_element_type=F32)
    o_ref[...] = _layer_norm(alpha * x1 + y, g2_ref[...], b2_ref[...])


def _finish_call(x, attn, ssm, gi, bi, wo, g1, b1, wr, br, wg, wu, wd, g2, b2, alpha, pre_ln, tm):
    n, d = x.shape
    row = lambda c: pl.BlockSpec((tm, c), lambda i: (i, 0))
    return pl.pallas_call(
        functools.partial(_finish_kernel, alpha=alpha, pre_ln=pre_ln),
        grid=(n // tm,),
        in_specs=[row(d), row(attn.shape[1]), row(ssm.shape[1]),
                  _resident(gi.shape), _resident(bi.shape), _resident(wo.shape),
                  _resident(g1.shape), _resident(b1.shape), _resident(wr.shape), _resident(br.shape),
                  _resident(wg.shape), _resident(wu.shape), _resident(wd.shape),
                  _resident(g2.shape), _resident(b2.shape)],
        out_specs=row(d),
        out_shape=jax.ShapeDtypeStruct((n, d), F32),
        compiler_params=_cparams(1),
        name="finish",
    )(x, attn, ssm, gi, bi, wo, g1, b1, wr, br, wg, wu, wd, g2, b2)


def _row(a):
    return a.reshape(1, -1).astype(F32)


def _block_diag(a):
    g, r, c = a.shape
    return jnp.einsum("grc,gh->grhc", a, jnp.eye(g, dtype=a.dtype)).reshape(g * r, g * c)


def _pad_lanes(a):
    return jnp.pad(a, ((0, 0), (0, 0), (0, LANES - a.shape[2])))


def _pick_tile(n, pref):
    t = min(n, pref)
    while n % t:
        t //= 2
    return t


def kernel(x_prompt, x_sample, cache_k, cache_v, cache_kidx, state_ssm_re, state_ssm_im, page_table, ln_in_g, ln_in_b, rel_bias, w_in, idx_ln_g, idx_ln_b, ssm_lam_re, ssm_lam_im, ssm_log_dt, ssm_b_re, ssm_b_im, ssm_c_re, ssm_c_im, ssm_d, w_glu, w_out, ln_mix_g, ln_mix_b, w_router_grp, b_router_grp, w_router_exp, b_router_exp, w_gate, w_up, w_down, ln_ffn_g, ln_ffn_b):
    bsz, seq, d_model = x_prompt.shape
    dbsz, dseq, _ = x_sample.shape
    depth = w_in.shape[0]
    d_attn = N_HEADS * HEAD_DIM
    n_grp = ssm_lam_re.shape[1]
    d_ssm = n_grp * SSM_GROUP
    n_state = n_grp * SSM_STATE
    n_pool, page = cache_k.shape[1], cache_k.shape[2]
    alpha = (2.0 * depth) ** 0.25
    sizes = (d_attn, d_attn, d_attn, N_IDX_HEADS * IDX_DIM, IDX_DIM, N_IDX_HEADS, d_ssm)
    cuts = [0] + [int(c) for c in np.cumsum(sizes)]
    rel_bias = rel_bias.astype(F32)
    page_table = page_table.astype(jnp.int32)

    xp = x_prompt.reshape(bsz * seq, d_model)
    xs = x_sample.reshape(dbsz * dseq, d_model)
    outs = [[] for _ in range(10)]
    for l in range(depth):
        pre_ln = l == 0
        seg = [w_in[l][:, cuts[j]:cuts[j + 1]] for j in range(7)]
        pad = jnp.zeros((d_model, LANES - N_IDX_HEADS), w_in.dtype)
        w_row = jnp.concatenate([seg[0], seg[3], seg[6], seg[5], pad], axis=1).astype(BF16)
        w_t = jnp.concatenate([seg[1], seg[2], seg[4]], axis=1).T.astype(BF16)
        proj_w = (_row(ln_in_g), _row(ln_in_b), w_row, w_t,
                  idx_ln_g[l].reshape(IDX_DIM, 1).astype(F32), idx_ln_b[l].reshape(IDX_DIM, 1).astype(F32))
        ssm_w = (_row(ssm_lam_re[l]), _row(ssm_lam_im[l]), _row(jnp.repeat(ssm_log_dt[l], SSM_STATE)),
                 _block_diag(jnp.swapaxes(ssm_b_re[l], 1, 2)).astype(F32),
                 _block_diag(jnp.swapaxes(ssm_b_im[l], 1, 2)).astype(F32),
                 _block_diag(jnp.swapaxes(ssm_c_re[l], 1, 2)).astype(BF16),
                 _block_diag(jnp.swapaxes(ssm_c_im[l], 1, 2)).astype(BF16),
                 _row(ssm_d[l]), w_glu[l].astype(BF16))
        wr = jnp.concatenate([w_router_grp[l], jnp.transpose(w_router_exp[l], (1, 0, 2)).reshape(d_model, N_EXPERTS)], axis=1)
        wr = jnp.pad(wr, ((0, 0), (0, LANES - wr.shape[1]))).astype(BF16)
        br = jnp.pad(jnp.concatenate([b_router_grp[l], b_router_exp[l].reshape(-1)]),
                     (0, LANES - N_EXPERT_GROUPS - N_EXPERTS)).reshape(1, LANES).astype(F32)
        fin_w = (_row(ln_in_g), _row(ln_in_b), w_out[l].astype(BF16), _row(ln_mix_g[l]), _row(ln_mix_b[l]), wr, br,
                 w_gate[l].astype(BF16), w_up[l].astype(BF16), w_down[l].astype(BF16),
                 _row(ln_ffn_g[l]), _row(ln_ffn_b[l]))

        attn_l, k_l, v_l, ki_l, u_l = [], [], [], [], []
        for b in range(bsz):
            q, qi, u, ws, kt, vt, kt16, vt16, kit, kit16 = _proj_call(
                xp[b * seq:(b + 1) * seq], *proj_w, d_attn, d_ssm, pre_ln, _pick_tile(seq, PROJ_ROWS))
            attn_l.append(_pattn_call(rel_bias, q[None], qi[None], ws[None], kit16[None], kt16[None], vt16[None],
                                      tq=QUERY_BLOCK, kb=KEY_CHUNK, scan_kb=SCAN_CHUNK)[0])
            k_l.append(jnp.transpose(kt.reshape(N_HEADS, HEAD_DIM, seq), (2, 0, 1)))
            v_l.append(jnp.transpose(vt.reshape(N_HEADS, HEAD_DIM, seq), (2, 0, 1)))
            ki_l.append(kit.T)
            u_l.append(u)
        u_tm = jnp.stack(u_l, axis=1).reshape(seq * bsz, d_ssm)
        h0 = jnp.zeros((bsz, n_state), F32)
        ssm_tm, hr, hi = _ssm_call(u_tm, h0, h0, *ssm_w, nb=bsz, tt=_pick_tile(seq, SSM_ROWS) * bsz)
        ssm_o = jnp.swapaxes(ssm_tm.reshape(seq, bsz, d_ssm), 0, 1).reshape(bsz * seq, d_ssm)
        xp = _finish_call(xp, jnp.concatenate(attn_l, axis=0), ssm_o, *fin_w, alpha=alpha, pre_ln=pre_ln,
                          tm=_pick_tile(bsz * seq, FINISH_ROWS))
        outs[0].append(jnp.stack(k_l))
        outs[1].append(jnp.stack(v_l))
        outs[2].append(jnp.stack(ki_l))
        outs[3].append(hr.reshape(bsz, n_grp, SSM_STATE).astype(state_ssm_re.dtype))
        outs[4].append(hi.reshape(bsz, n_grp, SSM_STATE).astype(state_ssm_im.dtype))

        n_s = dbsz * dseq
        q, qi, u, ws, kt, vt, kt16, vt16, kit, kit16 = _proj_call(xs, *proj_w, d_attn, d_ssm, pre_ln,
                                                                  _pick_tile(n_s, PROJ_ROWS))
        rows = N_IDX_HEADS * dseq
        qis = jnp.swapaxes(qi.reshape(dbsz, dseq, N_IDX_HEADS, IDX_DIM), 1, 2).reshape(dbsz, rows, IDX_DIM)
        wis = jnp.swapaxes(ws[:, :N_IDX_HEADS].reshape(dbsz, dseq, N_IDX_HEADS), 1, 2).reshape(dbsz, rows, 1)
        per_seq_t = lambda a: _pad_lanes(jnp.swapaxes(a.reshape(a.shape[0], dbsz, dseq), 0, 1))
        ck_t = jnp.transpose(cache_k[l], (0, 2, 3, 1)).reshape(n_pool, d_attn, page)
        cv_t = jnp.transpose(cache_v[l], (0, 2, 3, 1)).reshape(n_pool, d_attn, page)
        cki_t = jnp.swapaxes(cache_kidx[l], 1, 2)
        attn = _sattn_call(page_table, rel_bias, q.reshape(dbsz, dseq, d_attn), qis, wis,
                           per_seq_t(kit16[:IDX_DIM]), per_seq_t(kt16), per_seq_t(vt16), cki_t, ck_t, cv_t,
                           cp=min(PAGES_PER_CHUNK, page_table.shape[1]))
        u_tm = jnp.swapaxes(u.reshape(dbsz, dseq, d_ssm), 0, 1).reshape(n_s, d_ssm)
        ssm_tm, hr, hi = _ssm_call(u_tm, state_ssm_re[l].reshape(dbsz, n_state).astype(F32),
                                   state_ssm_im[l].reshape(dbsz, n_state).astype(F32), *ssm_w, nb=dbsz, tt=n_s)
        ssm_o = jnp.swapaxes(ssm_tm.reshape(dseq, dbsz, d_ssm), 0, 1).reshape(n_s, d_ssm)
        xs = _finish_call(xs, attn.reshape(n_s, d_attn), ssm_o, *fin_w, alpha=alpha, pre_ln=pre_ln,
                          tm=_pick_tile(n_s, FINISH_ROWS))
        outs[5].append(kt.T.reshape(dbsz, dseq, N_HEADS, HEAD_DIM))
        outs[6].append(vt.T.reshape(dbsz, dseq, N_HEADS, HEAD_DIM))
        outs[7].append(kit.T.reshape(dbsz, dseq, IDX_DIM))
        outs[8].append(hr.reshape(dbsz, n_grp, SSM_STATE).astype(state_ssm_re.dtype))
        outs[9].append(hi.reshape(dbsz, n_grp, SSM_STATE).astype(state_ssm_im.dtype))

    return (xp.reshape(bsz, seq, d_model), xs.reshape(dbsz, dseq, d_model)) + tuple(jnp.stack(o) for o in outs)
```
